```python
import math, functools
import jax, jax.numpy as jnp
from jax import lax
import numpy as np

D_MODEL = 1024
BATCH = 4
SEQ = 4096
DEPTH = 1
DEC_BATCH = 128
DEC_SEQ = 4
PAST_LEN = 8192
PAGE_SIZE = 128

N_HEADS = 8
QK_NOPE = 64
QK_ROPE = 32
QK_HEAD = QK_NOPE + QK_ROPE
V_HEAD = 64
Q_LORA = 384
KV_LORA = 256
ROPE_THETA = 10000.0
Q_BLOCK = 128
D_CONV = D_MODEL // 2
CONV_WIDTH = 31
D_FF = -(-8 * D_MODEL // (3 * 256)) * 256
D_IN = 2 * D_CONV + Q_LORA + KV_LORA + QK_ROPE + 2 * D_MODEL
EPS = 1e-6

kernel_name = "conformer_conv_mla_gated_hybrid_step"


def _rmsnorm(x, g):
    x32 = x.astype(jnp.float32)
    y = x32 * lax.rsqrt(jnp.mean(x32 * x32, axis=-1, keepdims=True) + EPS)
    return (y * g.astype(jnp.float32)).astype(x.dtype)


def _layernorm(x, g, b):
    x32 = x.astype(jnp.float32)
    mu = jnp.mean(x32, axis=-1, keepdims=True)
    xc = x32 - mu
    y = xc * lax.rsqrt(jnp.mean(xc * xc, axis=-1, keepdims=True) + EPS)
    return (y * g.astype(jnp.float32) + b.astype(jnp.float32)).astype(x.dtype)


def _head_norm(v, g_compact):
    g = jnp.concatenate([g_compact, g_compact[QK_NOPE:]], axis=0)
    return _rmsnorm(v, g)


def _rope_angles(pos):
    inv_freq = 1.0 / (ROPE_THETA ** (jnp.arange(0, QK_ROPE, 2, dtype=jnp.float32) / QK_ROPE))
    ang = pos.astype(jnp.float32)[:, None] * inv_freq[None, :]
    return jnp.cos(ang), jnp.sin(ang)


def _apply_rope(x, cos, sin):
    half = QK_ROPE // 2
    x1, x2 = x[..., :half], x[..., half:]
    c = cos.astype(x.dtype)
    s = sin.astype(x.dtype)
    return jnp.concatenate([x1 * c - x2 * s, x2 * c + x1 * s], axis=-1)


def _split_in(p):
    i0 = 2 * D_CONV
    i1 = i0 + Q_LORA
    i2 = i1 + KV_LORA
    i3 = i2 + QK_ROPE
    i4 = i3 + D_MODEL
    return p[..., :i0], p[..., i0:i1], p[..., i1:i2], p[..., i2:i3], p[..., i3:i4], p[..., i4:]


def _mla_keys(c_kv, k_pe, w_uk, k_norm_g):
    k_nope = jnp.einsum('...tc,chd->...thd', c_kv, w_uk)
    k_pe_h = jnp.broadcast_to(k_pe[..., None, :], k_nope.shape[:-1] + (QK_ROPE,))
    return _head_norm(jnp.concatenate([k_nope, k_pe_h], axis=-1), k_norm_g)


def _mla_prompt(q, c_kv, k_pe, w_uk, w_uv, k_norm_g):
    b, s = q.shape[0], q.shape[1]
    k = _mla_keys(c_kv, k_pe, w_uk, k_norm_g)
    v = jnp.einsum('btc,chd->bthd', c_kv, w_uv)
    nb = s // Q_BLOCK
    qb = q.reshape(b, nb, Q_BLOCK, N_HEADS, QK_HEAD).transpose(1, 0, 2, 3, 4)
    kpos = jnp.arange(s)
    scale = 1.0 / math.sqrt(QK_HEAD)

    def block(args):
        qi, i = args
        sc = jnp.einsum('bqhd,bkhd->bhqk', qi, k).astype(jnp.float32) * scale
        qpos = i * Q_BLOCK + jnp.arange(Q_BLOCK)
        sc = jnp.where(kpos[None, :] <= qpos[:, None], sc, -jnp.inf)
        p = jax.nn.softmax(sc, axis=-1).astype(v.dtype)
        return jnp.einsum('bhqk,bkhd->bqhd', p, v)

    o = lax.map(block, (qb, jnp.arange(nb)))
    return o.transpose(1, 0, 2, 3, 4).reshape(b, s, N_HEADS * V_HEAD)


def _mla_sample(q, c_kv, k_pe, cache_lat, cache_kpe, page_table, w_uk, w_uv, k_norm_g):
    n_pages = page_table.shape[1]
    past = n_pages * PAGE_SIZE
    t_new = q.shape[1]
    kpos = jnp.arange(past + t_new)
    qpos = past + jnp.arange(t_new)
    mask = kpos[None, :] <= qpos[:, None]
    scale = 1.0 / math.sqrt(QK_HEAD)

    def one(args):
        qi, ci, kpi, pt = args
        c_all = jnp.concatenate([cache_lat[pt].reshape(past, KV_LORA), ci], axis=0)
        kpe_all = jnp.concatenate([cache_kpe[pt].reshape(past, QK_ROPE), kpi], axis=0)
        k = _mla_keys(c_all, kpe_all, w_uk, k_norm_g)
        sc = jnp.einsum('qhd,khd->hqk', qi, k).astype(jnp.float32) * scale
        sc = jnp.where(mask[None], sc, -jnp.inf)
        p = jax.nn.softmax(sc, axis=-1).astype(c_all.dtype)
        o_lat = jnp.einsum('hqk,kc->qhc', p, c_all)
        return jnp.einsum('qhc,chd->qhd', o_lat, w_uv)

    o = lax.map(one, (q, c_kv, k_pe, page_table))
    return o.reshape(q.shape[0], t_new, N_HEADS * V_HEAD)


def _conv_branch(glu_in, conv_prev, conv_w, conv_b, ln_g, ln_b, w_conv_out):
    a, g = glu_in[..., :D_CONV], glu_in[..., D_CONV:]
    u = a * jax.nn.sigmoid(g)
    u_ext = jnp.concatenate([conv_prev.astype(u.dtype), u], axis=1)
    y = lax.conv_general_dilated(u_ext, conv_w[:, None, :].astype(u.dtype), window_strides=(1,),
                                 padding='VALID', dimension_numbers=('NWC', 'WIO', 'NWC'),
                                 feature_group_count=D_CONV) + conv_b
    y = jax.nn.silu(_layernorm(y, ln_g, ln_b))
    return y @ w_conv_out, u_ext[:, -(CONV_WIDTH - 1):]


def _layer(x, pos, conv_prev, attend, norm_mix_g, w_in, q_a_norm_g, w_uq, kv_a_norm_g, q_norm_g,
           w_o_mla, conv_w, conv_b, conv_ln_g, conv_ln_b, w_conv_out, w_out, norm_ffn_g,
           w_gate, w_up, w_down):
    n, t = x.shape[0], x.shape[1]
    h = _rmsnorm(x, norm_mix_g)
    glu_in, c_q, c_kv, k_pe, g_conv, g_mla = _split_in(h @ w_in)
    conv_out, conv_state = _conv_branch(glu_in, conv_prev, conv_w, conv_b, conv_ln_g, conv_ln_b, w_conv_out)
    cos, sin = _rope_angles(pos)
    q = (_rmsnorm(c_q, q_a_norm_g) @ w_uq).reshape(n, t, N_HEADS, QK_HEAD)
    q = jnp.concatenate([q[..., :QK_NOPE], _apply_rope(q[..., QK_NOPE:], cos[:, None, :], sin[:, None, :])], axis=-1)
    q = _head_norm(q, q_norm_g)
    c_kv = _rmsnorm(c_kv, kv_a_norm_g)
    k_pe = _apply_rope(k_pe, cos, sin)
    mla_out = attend(q, c_kv, k_pe) @ w_o_mla
    merged = jax.nn.sigmoid(g_conv) * conv_out + jax.nn.sigmoid(g_mla) * mla_out
    x = x + merged @ w_out
    h2 = _rmsnorm(x, norm_ffn_g)
    x = x + (jax.nn.silu(h2 @ w_gate) * (h2 @ w_up)) @ w_down
    return x, c_kv, k_pe, conv_state


def setup_inputs(seed: int = 0) -> dict:
    key = jax.random.key(seed)
    ks = jax.random.split(key, 32)
    n_pages = PAST_LEN // PAGE_SIZE
    n_used = DEC_BATCH * n_pages
    n_phys = n_used + n_used // 4
    f32 = jnp.float32

    def w(k, shape, fan_in):
        return jax.random.normal(k, shape, f32) * (fan_in ** -0.5)

    def gain(k, shape):
        return 1.0 + 0.05 * jax.random.normal(k, shape, f32)

    page_table = jax.random.permutation(ks[5], n_phys)[:n_used].reshape(DEC_BATCH, n_pages).astype(jnp.int32)
    return {
        "x_prompt": jax.random.normal(ks[0], (BATCH, SEQ, D_MODEL), f32),
        "x_sample": jax.random.normal(ks[1], (DEC_BATCH, DEC_SEQ, D_MODEL), f32),
        "cache_kv_latent": jax.random.normal(ks[2], (DEPTH, n_phys, PAGE_SIZE, KV_LORA), f32),
        "cache_k_rope": jax.random.normal(ks[3], (DEPTH, n_phys, PAGE_SIZE, QK_ROPE), f32),
        "state_conv": 0.5 * jax.random.normal(ks[4], (DEPTH, DEC_BATCH, CONV_WIDTH - 1, D_CONV), f32),
        "page_table": page_table,
        "norm_mix_g": gain(ks[6], (DEPTH, D_MODEL)),
        "w_in": w(ks[7], (DEPTH, D_MODEL, D_IN), D_MODEL),
        "q_a_norm_g": gain(ks[8], (DEPTH, Q_LORA)),
        "w_uq": w(ks[9], (DEPTH, Q_LORA, N_HEADS * QK_HEAD), Q_LORA),
        "kv_a_norm_g": gain(ks[10], (DEPTH, KV_LORA)),
        "w_uk": w(ks[11], (DEPTH, KV_LORA, N_HEADS, QK_NOPE), KV_LORA),
        "w_uv": w(ks[12], (DEPTH, KV_LORA, N_HEADS, V_HEAD), KV_LORA),
        "q_norm_g": gain(ks[13], (DEPTH, QK_NOPE + QK_ROPE // 2)),
        "k_norm_g": gain(ks[14], (DEPTH, QK_NOPE + QK_ROPE // 2)),
        "w_o_mla": w(ks[15], (DEPTH, N_HEADS * V_HEAD, D_MODEL), N_HEADS * V_HEAD),
        "conv_w": w(ks[16], (DEPTH, CONV_WIDTH, D_CONV), CONV_WIDTH),
        "conv_b": 0.02 * jax.random.normal(ks[17], (DEPTH, D_CONV), f32),
        "conv_ln_g": gain(ks[18], (DEPTH, D_CONV)),
        "conv_ln_b": 0.02 * jax.random.normal(ks[19], (DEPTH, D_CONV), f32),
        "w_conv_out": w(ks[20], (DEPTH, D_CONV, D_MODEL), D_CONV),
        "w_out": w(ks[21], (DEPTH, D_MODEL, D_MODEL), D_MODEL),
        "norm_ffn_g": gain(ks[22], (DEPTH, D_MODEL)),
        "w_gate": w(ks[23], (DEPTH, D_MODEL, D_FF), D_MODEL),
        "w_up": w(ks[24], (DEPTH, D_MODEL, D_FF), D_MODEL),
        "w_down": w(ks[25], (DEPTH, D_FF, D_MODEL), D_FF),
    }


def reference(x_prompt, x_sample, cache_kv_latent, cache_k_rope, state_conv, page_table,
              norm_mix_g, w_in, q_a_norm_g, w_uq, kv_a_norm_g, w_uk, w_uv, q_norm_g, k_norm_g,
              w_o_mla, conv_w, conv_b, conv_ln_g, conv_ln_b, w_conv_out, w_out, norm_ffn_g,
              w_gate, w_up, w_down):
    past = page_table.shape[1] * PAGE_SIZE
    pos_p = jnp.arange(x_prompt.shape[1])
    pos_s = past + jnp.arange(x_sample.shape[1])
    yp, ys = x_prompt, x_sample
    lat_p, kpe_p, conv_p, lat_s, kpe_s, conv_s = [], [], [], [], [], []
    for l in range(DEPTH):
        lw = (norm_mix_g[l], w_in[l], q_a_norm_g[l], w_uq[l], kv_a_norm_g[l], q_norm_g[l],
              w_o_mla[l], conv_w[l], conv_b[l], conv_ln_g[l], conv_ln_b[l], w_conv_out[l],
              w_out[l], norm_ffn_g[l], w_gate[l], w_up[l], w_down[l])
        attend_p = functools.partial(_mla_prompt, w_uk=w_uk[l], w_uv=w_uv[l], k_norm_g=k_norm_g[l])
        zeros_prev = jnp.zeros((yp.shape[0], CONV_WIDTH - 1, D_CONV), yp.dtype)
        yp, c1, k1, s1 = _layer(yp, pos_p, zeros_prev, attend_p, *lw)
        attend_s = functools.partial(_mla_sample, cache_lat=cache_kv_latent[l], cache_kpe=cache_k_rope[l],
                                     page_table=page_table, w_uk=w_uk[l], w_uv=w_uv[l], k_norm_g=k_norm_g[l])
        ys, c2, k2, s2 = _layer(ys, pos_s, state_conv[l], attend_s, *lw)
        lat_p.append(c1); kpe_p.append(k1); conv_p.append(s1)
        lat_s.append(c2); kpe_s.append(k2); conv_s.append(s2)
    return (yp, ys, jnp.stack(lat_p), jnp.stack(kpe_p), jnp.stack(conv_p),
            jnp.stack(lat_s), jnp.stack(kpe_s), jnp.stack(conv_s))
```

```python
import functools
import math

import jax
import jax.numpy as jnp
from jax import lax
from jax.experimental import pallas as pl
from jax.experimental.pallas import tpu as pltpu

N_HEADS = 8
QK_NOPE = 64
QK_ROPE = 32
ROPE_HALF = QK_ROPE // 2
QK_HEAD = QK_NOPE + QK_ROPE
V_HEAD = 64
ROPE_THETA = 10000.0
EPS = 1e-6
PAGE_SIZE = 128

LANES = 128
HEAD_TILE = LANES
NOPE_OFF = QK_ROPE
VMEM_LIMIT_BYTES = 56 * 1024 * 1024

F32 = jnp.float32
BF16 = jnp.bfloat16
NEG_BIG = -1e30


def _dot(a, b):
    return jnp.dot(a, b, preferred_element_type=F32)


def _dot_nt(a, b):
    return lax.dot_general(a, b, (((1,), (1,)), ((), ())), preferred_element_type=F32)


def _rms_scale(x):
    return lax.rsqrt(jnp.mean(x * x, axis=-1, keepdims=True) + EPS)


def _const_spec(shape, single_buffer=False):
    nd = len(shape)
    if single_buffer:
        return pl.BlockSpec(shape, lambda *_: (0,) * nd, pipeline_mode=pl.Buffered(1))
    return pl.BlockSpec(shape, lambda *_: (0,) * nd)


def _rope_tile(t, c_tab, sl_tab, sr_tab):
    return t * c_tab + pltpu.roll(t, ROPE_HALF, 1) * sr_tab + pltpu.roll(t, LANES - ROPE_HALF, 1) * sl_tab


def _in_proj_kernel(x_ref, ctab_ref, sltab_ref, srtab_ref, gmix_ref, w_in_ref, gqa_ref, w_uq_ref,
                    gkva_ref, gq_ref, gk_ref, w_uk_ref, w_uv_ref,
                    u_ref, q_ref, lat_ref, kpe_ref, gates_ref, *kv_refs,
                    d_conv, q_lora, kv_lora, d_model, with_kv):
    x = x_ref[...]
    hb = (x * _rms_scale(x) * gmix_ref[...]).astype(BF16)
    c_tab, sl_tab, sr_tab = ctab_ref[...], sltab_ref[...], srtab_ref[...]

    o_cq = 2 * d_conv
    o_ckv = o_cq + q_lora
    o_kpe = o_ckv + kv_lora
    o_gate = o_kpe + HEAD_TILE

    ag = _dot(hb, w_in_ref[:, 0:o_cq])
    u_ref[...] = ag[:, :d_conv] * jax.nn.sigmoid(ag[:, d_conv:])

    cq = _dot(hb, w_in_ref[:, o_cq:o_ckv])
    cqn = (cq * _rms_scale(cq) * gqa_ref[...]).astype(BF16)
    q = _dot(cqn, w_uq_ref[...])
    q_scale = 1.0 / math.sqrt(QK_HEAD)
    for h in range(N_HEADS):
        sl = slice(h * HEAD_TILE, (h + 1) * HEAD_TILE)
        qh = _rope_tile(q[:, sl], c_tab, sl_tab, sr_tab)
        ss = jnp.sum(qh * qh, axis=-1, keepdims=True)
        qh = qh * lax.rsqrt(ss * (1.0 / QK_HEAD) + EPS) * (gq_ref[...] * q_scale)
        q_ref[:, sl] = qh.astype(q_ref.dtype)

    ck = _dot(hb, w_in_ref[:, o_ckv:o_gate])
    ckv = ck[:, :kv_lora]
    lat = ckv * _rms_scale(ckv) * gkva_ref[...]
    lat_ref[...] = lat
    kpe = _rope_tile(ck[:, kv_lora:], c_tab, sl_tab, sr_tab)
    kpe_ref[...] = kpe[:, :QK_ROPE]

    for g in range(2):
        sl = slice(g * d_model, (g + 1) * d_model)
        gates_ref[:, sl] = jax.nn.sigmoid(_dot(hb, w_in_ref[:, o_gate + g * d_model:o_gate + (g + 1) * d_model]))

    if with_kv:
        k_ref, v_ref = kv_refs
        latb = lat.astype(BF16)
        v_ref[...] = _dot(latb, w_uv_ref[...]).astype(v_ref.dtype)
        kn = _dot(latb, w_uk_ref[...])
        for h in range(N_HEADS):
            sl = slice(h * HEAD_TILE, (h + 1) * HEAD_TILE)
            kh = kn[:, sl] + kpe
            ss = jnp.sum(kh * kh, axis=-1, keepdims=True)
            k_ref[:, sl] = (kh * lax.rsqrt(ss * (1.0 / QK_HEAD) + EPS) * gk_ref[...]).astype(k_ref.dtype)


def _in_proj(x, tabs, wts, *, tm, with_kv):
    t, d_model = x.shape
    n_tab_tiles = tabs[0].shape[0] // tm
    d_in_p = wts["w_in"].shape[1]
    q_lora = wts["w_uq"].shape[0]
    kv_lora = wts["w_uk"].shape[0]
    d_conv = (d_in_p - q_lora - kv_lora - HEAD_TILE - 2 * d_model) // 2
    qw = N_HEADS * HEAD_TILE
    vw = N_HEADS * V_HEAD

    row = lambda w: pl.BlockSpec((tm, w), lambda i: (i, 0))
    tab = pl.BlockSpec((tm, LANES), lambda i: (i % n_tab_tiles, 0))
    in_specs = [row(d_model), tab, tab, tab,
                _const_spec((1, d_model)), _const_spec(wts["w_in"].shape),
                _const_spec((1, q_lora)), _const_spec(wts["w_uq"].shape),
                _const_spec((1, kv_lora)), _const_spec((1, HEAD_TILE)), _const_spec((1, HEAD_TILE)),
                _const_spec(wts["w_uk"].shape), _const_spec(wts["w_uv"].shape)]
    out_shape = [jax.ShapeDtypeStruct((t, d_conv), F32), jax.ShapeDtypeStruct((t, qw), BF16),
                 jax.ShapeDtypeStruct((t, kv_lora), F32), jax.ShapeDtypeStruct((t, QK_ROPE), F32),
                 jax.ShapeDtypeStruct((t, 2 * d_model), F32)]
    out_specs = [row(d_conv), row(qw), row(kv_lora), row(QK_ROPE), row(2 * d_model)]
    if with_kv:
        out_shape += [jax.ShapeDtypeStruct((t, qw), BF16), jax.ShapeDtypeStruct((t, vw), BF16)]
        out_specs += [row(qw), row(vw)]
    kern = functools.partial(_in_proj_kernel, d_conv=d_conv, q_lora=q_lora, kv_lora=kv_lora,
                             d_model=d_model, with_kv=with_kv)
    return pl.pallas_call(
        kern, grid=(t // tm,), in_specs=in_specs, out_specs=out_specs, out_shape=out_shape,
        compiler_params=pltpu.CompilerParams(dimension_semantics=("arbitrary",),
                                             vmem_limit_bytes=VMEM_LIMIT_BYTES),
        name="in_proj_kv" if with_kv else "in_proj",
    )(x, *tabs, wts["g_mix"], wts["w_in"], wts["g_qa"], wts["w_uq"], wts["g_kva"], wts["g_q"], wts["g_k"],
      wts["w_uk"], wts["w_uv"])


def _ln_silu(y, g, b):
    mu = jnp.mean(y, axis=-1, keepdims=True)
    yc = y - mu
    yn = yc * lax.rsqrt(jnp.mean(yc * yc, axis=-1, keepdims=True) + EPS) * g + b
    return yn * jax.nn.sigmoid(yn)


HALO = 32


def _conv_prompt_kernel(cur_ref, prev_ref, w_ref, b_ref, lng_ref, lnb_ref, y_ref, ext_ref, *, width, rows):
    ts = cur_ref.shape[1]
    ext_ref[HALO:, :] = cur_ref[0]
    prev = prev_ref[0]
    ext_ref[0:HALO, :] = jnp.where(pl.program_id(1) > 0, prev, jnp.zeros_like(prev))
    base = HALO - (width - 1)
    for r0 in range(0, ts, rows):
        acc = ext_ref[base + r0:base + r0 + rows, :] * w_ref[0:1, :]
        for k in range(1, width):
            acc = acc + ext_ref[base + r0 + k:base + r0 + k + rows, :] * w_ref[k:k + 1, :]
        y = _ln_silu(acc + b_ref[...], lng_ref[...], lnb_ref[...])
        y_ref[r0:r0 + rows, :] = y.astype(y_ref.dtype)


def _conv_prompt(u, conv_w, conv_b, ln_g, ln_b, *, batch, ts, rows=64):
    t, d_conv = u.shape
    seq = t // batch
    width = conv_w.shape[0]
    assert width - 1 <= HALO and ts % HALO == 0 and seq % ts == 0
    u3 = u.reshape(batch, seq, d_conv)
    nst = seq // ts
    kern = functools.partial(_conv_prompt_kernel, width=width, rows=rows)
    return pl.pallas_call(
        kern, grid=(batch, nst),
        in_specs=[pl.BlockSpec((1, ts, d_conv), lambda b, i: (b, i, 0)),
                  pl.BlockSpec((1, HALO, d_conv), lambda b, i: (b, jnp.maximum(i * (ts // HALO) - 1, 0), 0)),
                  _const_spec(conv_w.shape), _const_spec((1, d_conv)), _const_spec((1, d_conv)),
                  _const_spec((1, d_conv))],
        out_specs=pl.BlockSpec((ts, d_conv), lambda b, i: (b * nst + i, 0)),
        out_shape=jax.ShapeDtypeStruct((t, d_conv), BF16),
        scratch_shapes=[pltpu.VMEM((ts + HALO, d_conv), F32)],
        compiler_params=pltpu.CompilerParams(dimension_semantics=("arbitrary", "arbitrary"),
                                             vmem_limit_bytes=VMEM_LIMIT_BYTES),
        name="conv_prompt",
    )(u3, u3, conv_w, conv_b, ln_g, ln_b)


def _conv_decode_kernel(state_ref, unew_ref, wa_ref, wb_ref, b_ref, lng_ref, lnb_ref, y_ref):
    state = state_ref[...]
    unew = unew_ref[...]
    for t in range(unew.shape[1]):
        y = jnp.sum(state * wa_ref[t][None], axis=1) + jnp.sum(unew * wb_ref[t][None], axis=1)
        y_ref[t] = _ln_silu(y + b_ref[...], lng_ref[...], lnb_ref[...]).astype(y_ref.dtype)


def _conv_decode(state, u_new, wa, wb, conv_b, ln_g, ln_b):
    nb, t_new, d_conv = u_new.shape
    full = lambda a: _const_spec(a.shape)
    return pl.pallas_call(
        _conv_decode_kernel, grid=(1,),
        in_specs=[full(state), full(u_new), full(wa), full(wb), _const_spec((1, d_conv)),
                  _const_spec((1, d_conv)), _const_spec((1, d_conv))],
        out_specs=_const_spec((t_new, nb, d_conv)),
        out_shape=jax.ShapeDtypeStruct((t_new, nb, d_conv), BF16),
        compiler_params=pltpu.CompilerParams(dimension_semantics=("arbitrary",),
                                             vmem_limit_bytes=VMEM_LIMIT_BYTES),
        name="conv_decode",
    )(state, u_new, wa, wb, conv_b, ln_g, ln_b)


def _flash_kernel(qi_tab, ki_tab, q_ref, k_ref, v_ref, o_ref, m_ref, l_ref, acc_ref):
    step = pl.program_id(2)
    qi = qi_tab[step]
    ki = ki_tab[step]
    tq = q_ref.shape[1]
    tk = k_ref.shape[1]

    @pl.when(ki == 0)
    def _():
        m_ref[...] = jnp.full_like(m_ref, NEG_BIG)
        l_ref[...] = jnp.zeros_like(l_ref)
        acc_ref[...] = jnp.zeros_like(acc_ref)

    row = qi * tq + lax.broadcasted_iota(jnp.int32, (tq, tk), 0)
    col = ki * tk + lax.broadcasted_iota(jnp.int32, (tq, tk), 1)
    visible = col <= row
    lane = lax.broadcasted_iota(jnp.int32, (tq, 2 * V_HEAD), 1)
    first = lane < V_HEAD
    v = v_ref[0]
    pvs, alphas = [], []
    for hh in range(2):
        sl = slice(hh * HEAD_TILE, (hh + 1) * HEAD_TILE)
        s = jnp.where(visible, _dot_nt(q_ref[0, :, sl], k_ref[0, :, sl]), NEG_BIG)
        m_old = m_ref[hh]
        m_new = jnp.maximum(m_old, jnp.max(s, axis=-1, keepdims=True))
        alpha = jnp.exp(m_old - m_new)
        p = jnp.exp(s - m_new)
        l_ref[hh] = alpha * l_ref[hh] + jnp.sum(p, axis=-1, keepdims=True)
        m_ref[hh] = m_new
        pvs.append(_dot(p.astype(v.dtype), v))
        alphas.append(alpha)
    acc_ref[...] = acc_ref[...] * jnp.where(first, alphas[0], alphas[1]) + jnp.where(first, pvs[0], pvs[1])

    @pl.when(ki == qi)
    def _():
        o_ref[0] = (acc_ref[...] / jnp.where(first, l_ref[0], l_ref[1])).astype(o_ref.dtype)


def _flash_attention(q, k, v, *, batch, tq):
    t, qw = q.shape
    seq = t // batch
    nq = seq // tq
    pairs = N_HEADS // 2
    q3, k3, v3 = (a.reshape(batch, seq, a.shape[1]) for a in (q, k, v))
    qi_list = [qi for qi in range(nq) for _ in range(qi + 1)]
    ki_list = [ki for qi in range(nq) for ki in range(qi + 1)]
    qi_tab = jnp.asarray(qi_list, jnp.int32)
    ki_tab = jnp.asarray(ki_list, jnp.int32)
    grid_spec = pltpu.PrefetchScalarGridSpec(
        num_scalar_prefetch=2, grid=(batch, pairs, len(qi_list)),
        in_specs=[pl.BlockSpec((1, tq, 2 * HEAD_TILE), lambda b, hp, s, qt, kt: (b, qt[s], hp)),
                  pl.BlockSpec((1, tq, 2 * HEAD_TILE), lambda b, hp, s, qt, kt: (b, kt[s], hp)),
                  pl.BlockSpec((1, tq, 2 * V_HEAD), lambda b, hp, s, qt, kt: (b, kt[s], hp))],
        out_specs=pl.BlockSpec((1, tq, 2 * V_HEAD), lambda b, hp, s, qt, kt: (b, qt[s], hp)),
        scratch_shapes=[pltpu.VMEM((2, tq, 1), F32), pltpu.VMEM((2, tq, 1), F32),
                        pltpu.VMEM((tq, 2 * V_HEAD), F32)])
    out = pl.pallas_call(
        _flash_kernel, grid_spec=grid_spec,
        out_shape=jax.ShapeDtypeStruct((batch, seq, N_HEADS * V_HEAD), BF16),
        compiler_params=pltpu.CompilerParams(dimension_semantics=("arbitrary",) * 3,
                                             vmem_limit_bytes=VMEM_LIMIT_BYTES),
        name="flash_prompt",
    )(qi_tab, ki_tab, q3, k3, v3)
    return out.reshape(t, N_HEADS * V_HEAD)


def _paged_kernel(pt_ref, q_ref, latn_ref, kpen_ref, gk_ref, wabs_ref, wukt_ref, wuv_ref, lat_hbm, kpe_hbm,
                  o_ref, lat_buf, kpe_buf, sems, cb_ref, cbn_ref, kpn_ref, s_ref, lhs_ref, *, layer, n_pages, tk):
    b = pl.program_id(0)
    nb = pl.num_programs(0)
    past = n_pages * PAGE_SIZE
    n_chunks = past // tk
    kv_lora = lat_buf.shape[2]
    t_new = q_ref.shape[1]
    rows = t_new * N_HEADS

    def page_copies(seq, slot, j):
        page = pt_ref[seq * n_pages + j]
        dst = pl.ds(j * PAGE_SIZE, PAGE_SIZE)
        return (pltpu.make_async_copy(lat_hbm.at[layer, page], lat_buf.at[slot, dst], sems.at[slot, 0]),
                pltpu.make_async_copy(kpe_hbm.at[layer, page], kpe_buf.at[slot, dst], sems.at[slot, 1]))

    def start_fetch(seq, slot):
        for j in range(n_pages):
            for cp in page_copies(seq, slot, j):
                cp.start()

    def wait_fetch(seq, slot):
        for j in range(n_pages):
            for cp in page_copies(seq, slot, j):
                cp.wait()

    slot = b % 2

    @pl.when(b == 0)
    def _():
        start_fetch(0, 0)
        lhs_ref[0:N_HEADS * QK_NOPE, :] = wukt_ref[...]
        cbn_ref[...] = jnp.zeros_like(cbn_ref)
        kpn_ref[...] = jnp.zeros_like(kpn_ref)

    @pl.when(b + 1 < nb)
    def _():
        start_fetch(b + 1, 1 - slot)

    qg = q_ref[0].astype(F32) * gk_ref[...]
    head_of_lane = lax.broadcasted_iota(jnp.int32, (N_HEADS, N_HEADS * HEAD_TILE), 1) // HEAD_TILE
    head_of_row = lax.broadcasted_iota(jnp.int32, (N_HEADS, N_HEADS * HEAD_TILE), 0)
    head_mask = head_of_lane == head_of_row
    qexp = jnp.concatenate(
        [jnp.where(head_mask, jnp.broadcast_to(qg[i:i + 1, :], head_mask.shape), 0.0)
         for i in range(t_new)], axis=0)
    qall = _dot(qexp.astype(BF16), wabs_ref[...])
    lhs_ref[N_HEADS * QK_NOPE:, :] = qall[:, :kv_lora].astype(BF16)
    qpe = qall[:, kv_lora:].astype(BF16)
    ones = jnp.ones((N_HEADS, QK_ROPE), BF16)

    def scores(cb, kp):
        r = _dot_nt(lhs_ref[...], cb)
        n = cb.shape[0]
        kn = r[:N_HEADS * QK_NOPE]
        ss = jnp.sum((kn * kn).reshape(N_HEADS, QK_NOPE, n), axis=1)
        sq = kp * kp
        hi = sq.astype(BF16)
        lo = (sq - hi.astype(F32)).astype(BF16)
        ss = ss + _dot_nt(ones, hi) + _dot_nt(ones, lo)
        inv = lax.rsqrt(ss * (1.0 / QK_HEAD) + EPS)
        raw = r[N_HEADS * QK_NOPE:] + _dot_nt(qpe, kp.astype(BF16))
        return raw * jnp.concatenate([inv] * t_new, axis=0)

    wait_fetch(b, slot)

    def chunk(c, m):
        st = pl.multiple_of(c * tk, tk)
        cb = lat_buf[slot, pl.ds(st, tk), :].astype(BF16)
        cb_ref[pl.ds(st, tk), :] = cb
        sc = scores(cb, kpe_buf[slot, pl.ds(st, tk), :])
        s_ref[c] = sc
        return jnp.maximum(m, jnp.max(sc, axis=-1, keepdims=True))

    m = lax.fori_loop(0, n_chunks, chunk, jnp.full((rows, 1), NEG_BIG, F32))

    cbn_ref[0:t_new, :] = latn_ref[0].astype(BF16)
    kpn_ref[0:t_new, :] = kpen_ref[0]
    cbn = cbn_ref[...]
    col = lax.broadcasted_iota(jnp.int32, (rows, LANES), 1)
    qrow = lax.broadcasted_iota(jnp.int32, (rows, LANES), 0) // N_HEADS
    s_new = jnp.where(col <= qrow, scores(cbn, kpn_ref[...]), NEG_BIG)
    m = jnp.maximum(m, jnp.max(s_new, axis=-1, keepdims=True))

    p_new = jnp.exp(s_new - m)
    l0 = jnp.sum(p_new, axis=-1, keepdims=True)
    o0 = _dot(p_new.astype(BF16), cbn)

    def pv(c, carry):
        l, o = carry
        st = pl.multiple_of(c * tk, tk)
        p = jnp.exp(s_ref[c] - m)
        return (l + jnp.sum(p, axis=-1, keepdims=True), o + _dot(p.astype(BF16), cb_ref[pl.ds(st, tk), :]))

    l, o = lax.fori_loop(0, n_chunks, pv, (l0, o0))
    o_lat = (o / l).astype(BF16)
    full = _dot(o_lat, wuv_ref[...])
    vlane = lax.broadcasted_iota(jnp.int32, (rows, N_HEADS * V_HEAD), 1) // V_HEAD
    vrow = lax.broadcasted_iota(jnp.int32, (rows, N_HEADS * V_HEAD), 0) % N_HEADS
    own = jnp.where(vlane == vrow, full, 0.0).astype(BF16)
    fold = (lax.broadcasted_iota(jnp.int32, (t_new, rows), 1) // N_HEADS
            == lax.broadcasted_iota(jnp.int32, (t_new, rows), 0))
    o_ref[0] = _dot(jnp.where(fold, 1.0, 0.0).astype(BF16), own).astype(o_ref.dtype)


def _paged_attention(page_table, q, lat_new, kpe_new, cache_lat, cache_kpe, gk_full, wabs, wukt, wuv, *,
                     layer, tk):
    nb, n_pages = page_table.shape
    t_new = q.shape[0] // nb
    kv_lora = cache_lat.shape[-1]
    past = n_pages * PAGE_SIZE
    rows = t_new * N_HEADS
    q3 = q.reshape(nb, t_new, q.shape[1])
    ln3 = lat_new.reshape(nb, t_new, kv_lora)
    kn3 = kpe_new.reshape(nb, t_new, QK_ROPE)
    per_seq = lambda w: pl.BlockSpec((1, t_new, w), lambda b, pt: (b, 0, 0))
    const = lambda a: pl.BlockSpec(a.shape, lambda b, pt: (0,) * a.ndim)
    grid_spec = pltpu.PrefetchScalarGridSpec(
        num_scalar_prefetch=1, grid=(nb,),
        in_specs=[per_seq(q.shape[1]), per_seq(kv_lora), per_seq(QK_ROPE), const(gk_full), const(wabs),
                  const(wukt), const(wuv), pl.BlockSpec(memory_space=pl.ANY),
                  pl.BlockSpec(memory_space=pl.ANY)],
        out_specs=pl.BlockSpec((1, t_new, N_HEADS * V_HEAD), lambda b, pt: (b, 0, 0)),
        scratch_shapes=[pltpu.VMEM((2, past, kv_lora), F32), pltpu.VMEM((2, past, QK_ROPE), F32),
                        pltpu.SemaphoreType.DMA((2, 2)),
                        pltpu.VMEM((past, kv_lora), BF16), pltpu.VMEM((LANES, kv_lora), BF16),
                        pltpu.VMEM((LANES, QK_ROPE), F32), pltpu.VMEM((past // tk, rows, tk), F32),
                        pltpu.VMEM((N_HEADS * QK_NOPE + rows, kv_lora), BF16)])
    kern = functools.partial(_paged_kernel, layer=layer, n_pages=n_pages, tk=tk)
    out = pl.pallas_call(
        kern, grid_spec=grid_spec,
        out_shape=jax.ShapeDtypeStruct((nb, t_new, N_HEADS * V_HEAD), BF16),
        compiler_params=pltpu.CompilerParams(dimension_semantics=("arbitrary",),
                                             vmem_limit_bytes=VMEM_LIMIT_BYTES),
        name="paged_decode",
    )(page_table.reshape(-1), q3, ln3, kn3, gk_full, wabs, wukt, wuv, cache_lat, cache_kpe)
    return out.reshape(nb * t_new, N_HEADS * V_HEAD)


FF_CHUNK = 256


def _out_ffn_kernel(x_ref, attn_ref, y_ref, gates_ref, wco_ref, womla_ref, wout_ref, gffn_ref,
                    wg_ref, wu_ref, wd_ref, o_ref, act_ref):
    d_model = x_ref.shape[1]
    d_ff = wg_ref.shape[1]
    conv_out = _dot(y_ref[...], wco_ref[...])
    mla_out = _dot(attn_ref[...], womla_ref[...])
    merged = gates_ref[:, :d_model] * conv_out + gates_ref[:, d_model:] * mla_out
    x1 = x_ref[...] + _dot(merged.astype(BF16), wout_ref[...])
    h2 = (x1 * _rms_scale(x1) * gffn_ref[...]).astype(BF16)
    for c0 in range(0, d_ff, FF_CHUNK):
        sl = slice(c0, c0 + FF_CHUNK)
        g = _dot(h2, wg_ref[:, sl])
        act_ref[:, sl] = (g * jax.nn.sigmoid(g) * _dot(h2, wu_ref[:, sl])).astype(BF16)
    o_ref[...] = x1 + _dot(act_ref[...], wd_ref[...])


def _out_ffn(x, attn, y_act, gates, wts, *, tm):
    t, d_model = x.shape
    d_ff = wts["w_gate"].shape[1]
    assert d_ff % FF_CHUNK == 0
    row = lambda w: pl.BlockSpec((tm, w), lambda i: (i, 0))
    wspec = lambda a: _const_spec(a.shape, single_buffer=True)
    return pl.pallas_call(
        _out_ffn_kernel, grid=(t // tm,),
        in_specs=[row(d_model), row(attn.shape[1]), row(y_act.shape[1]), row(2 * d_model),
                  wspec(wts["w_conv_out"]), wspec(wts["w_o_mla"]), wspec(wts["w_out"]),
                  _const_spec((1, d_model)), wspec(wts["w_gate"]), wspec(wts["w_up"]), wspec(wts["w_down"])],
        out_specs=row(d_model),
        out_shape=jax.ShapeDtypeStruct((t, d_model), F32),
        scratch_shapes=[pltpu.VMEM((tm, d_ff), BF16)],
        compiler_params=pltpu.CompilerParams(dimension_semantics=("arbitrary",),
                                             vmem_limit_bytes=VMEM_LIMIT_BYTES),
        name="out_ffn",
    )(x, attn, y_act, gates, wts["w_conv_out"], wts["w_o_mla"], wts["w_out"], wts["g_ffn"],
      wts["w_gate"], wts["w_up"], wts["w_down"])


def _head_tile_cols(w_rope, w_nope):
    lead = w_nope.shape[:-2]
    pad = jnp.zeros(lead + (N_HEADS, HEAD_TILE - QK_HEAD), w_nope.dtype)
    tile = jnp.concatenate([w_rope, w_nope, pad], axis=-1)
    return tile.reshape(lead + (N_HEADS * HEAD_TILE,))


def _gain_tile(g_compact):
    g = g_compact.astype(F32)
    return jnp.concatenate([g[QK_NOPE:], g[QK_NOPE:], g[:QK_NOPE], jnp.zeros((HEAD_TILE - QK_HEAD,), F32)])[None]


def _rope_tables(pos):
    inv_freq = 1.0 / (ROPE_THETA ** (jnp.arange(0, QK_ROPE, 2, dtype=F32) / QK_ROPE))
    ang = pos.astype(F32)[:, None] * inv_freq[None, :]
    c, s = jnp.cos(ang), jnp.sin(ang)
    n = pos.shape[0]
    z16 = jnp.zeros((n, ROPE_HALF), F32)
    rest = LANES - QK_ROPE
    c_tab = jnp.concatenate([c, c, jnp.ones((n, QK_NOPE), F32), jnp.zeros((n, rest - QK_NOPE), F32)], axis=1)
    sl_tab = jnp.concatenate([-s, z16, jnp.zeros((n, rest), F32)], axis=1)
    sr_tab = jnp.concatenate([z16, s, jnp.zeros((n, rest), F32)], axis=1)
    return c_tab, sl_tab, sr_tab


def _layer_weights(l, norm_mix_g, w_in, q_a_norm_g, w_uq, kv_a_norm_g, w_uk, w_uv, q_norm_g, k_norm_g,
                   w_o_mla, w_conv_out, w_out, norm_ffn_g, w_gate, w_up, w_down):
    d_model = w_in.shape[1]
    kv_lora = w_uk.shape[1]
    q_lora = w_uq.shape[1]
    d_conv = w_conv_out.shape[1]
    o1 = 2 * d_conv
    o2 = o1 + q_lora
    o3 = o2 + kv_lora
    o4 = o3 + QK_ROPE
    wi = w_in[l]
    w_in_p = jnp.concatenate([wi[:, :o3], wi[:, o3:o4], jnp.zeros((d_model, HEAD_TILE - QK_ROPE), wi.dtype),
                              wi[:, o4:]], axis=1).astype(BF16)
    uq = w_uq[l].reshape(q_lora, N_HEADS, QK_HEAD)
    w_uq_p = _head_tile_cols(uq[..., QK_NOPE:], uq[..., :QK_NOPE]).astype(BF16)
    uk = w_uk[l]
    w_uk_p = _head_tile_cols(jnp.zeros((kv_lora, N_HEADS, QK_ROPE), uk.dtype), uk).astype(BF16)
    w_ukt = uk.reshape(kv_lora, N_HEADS * QK_NOPE).T.astype(BF16)
    eye = jnp.eye(QK_ROPE, dtype=uk.dtype)
    per_head = []
    for h in range(N_HEADS):
        rope_rows = jnp.concatenate([jnp.zeros((QK_ROPE, kv_lora), uk.dtype), eye], axis=1)
        nope_rows = jnp.concatenate([uk[:, h, :].T, jnp.zeros((QK_NOPE, QK_ROPE), uk.dtype)], axis=1)
        pad_rows = jnp.zeros((HEAD_TILE - QK_HEAD, kv_lora + QK_ROPE), uk.dtype)
        per_head.append(jnp.concatenate([rope_rows, nope_rows, pad_rows], axis=0))
    w_abs = jnp.concatenate(per_head, axis=0).astype(BF16)
    g_k = _gain_tile(k_norm_g[l])
    return dict(
        g_mix=norm_mix_g[l][None].astype(F32), w_in=w_in_p, g_qa=q_a_norm_g[l][None].astype(F32), w_uq=w_uq_p,
        g_kva=kv_a_norm_g[l][None].astype(F32), g_q=_gain_tile(q_norm_g[l]), g_k=g_k,
        g_k_full=jnp.tile(g_k, (1, N_HEADS)), w_uk=w_uk_p, w_uv=w_uv[l].reshape(kv_lora, -1).astype(BF16),
        w_ukt=w_ukt, w_abs=w_abs, w_conv_out=w_conv_out[l].astype(BF16), w_o_mla=w_o_mla[l].astype(BF16),
        w_out=w_out[l].astype(BF16), g_ffn=norm_ffn_g[l][None].astype(F32), w_gate=w_gate[l].astype(BF16),
        w_up=w_up[l].astype(BF16), w_down=w_down[l].astype(BF16))


def _decode_conv_weights(conv_w, n_state, t_new):
    width, d_conv = conv_w.shape
    wa, wb = [], []
    for t in range(t_new):
        wa.append(jnp.concatenate([jnp.zeros((t, d_conv), conv_w.dtype), conv_w[:n_state - t]], axis=0))
        wb.append(jnp.concatenate([conv_w[width - 1 - t:], jnp.zeros((t_new - 1 - t, d_conv), conv_w.dtype)],
                                  axis=0))
    return jnp.stack(wa), jnp.stack(wb)


def kernel(x_prompt, x_sample, cache_kv_latent, cache_k_rope, state_conv, page_table, norm_mix_g, w_in,
           q_a_norm_g, w_uq, kv_a_norm_g, w_uk, w_uv, q_norm_g, k_norm_g, w_o_mla, conv_w, conv_b, conv_ln_g,
           conv_ln_b, w_conv_out, w_out, norm_ffn_g, w_gate, w_up, w_down):
    batch, seq, d_model = x_prompt.shape
    nb, t_new, _ = x_sample.shape
    depth = w_in.shape[0]
    n_pages = page_table.shape[1]
    past = n_pages * PAGE_SIZE
    n_state = state_conv.shape[2]
    d_conv = state_conv.shape[3]
    kv_lora = cache_kv_latent.shape[-1]

    tabs_p = _rope_tables(jnp.arange(seq))
    tabs_s = _rope_tables(jnp.tile(past + jnp.arange(t_new), nb))

    yp = x_prompt.reshape(batch * seq, d_model)
    ys = x_sample.reshape(nb * t_new, d_model)
    outs = [[] for _ in range(6)]
    for l in range(depth):
        wts = _layer_weights(l, norm_mix_g, w_in, q_a_norm_g, w_uq, kv_a_norm_g, w_uk, w_uv, q_norm_g,
                             k_norm_g, w_o_mla, w_conv_out, w_out, norm_ffn_g, w_gate, w_up, w_down)
        cb, lg, lb = (a[l][None].astype(F32) for a in (conv_b, conv_ln_g, conv_ln_b))

        u, q, lat, kpe, gates, k, v = _in_proj(yp, tabs_p, wts, tm=512, with_kv=True)
        y_act = _conv_prompt(u, conv_w[l].astype(F32), cb, lg, lb, batch=batch, ts=512)
        attn = _flash_attention(q, k, v, batch=batch, tq=512)
        yp = _out_ffn(yp, attn, y_act, gates, wts, tm=512)
        outs[0].append(lat.reshape(batch, seq, kv_lora))
        outs[1].append(kpe.reshape(batch, seq, QK_ROPE))
        outs[2].append(u.reshape(batch, seq, d_conv)[:, seq - n_state:])

        u, q, lat, kpe, gates = _in_proj(ys, tabs_s, wts, tm=nb * t_new, with_kv=False)
        u3 = u.reshape(nb, t_new, d_conv)
        wa, wb = _decode_conv_weights(conv_w[l].astype(F32), n_state, t_new)
        y_act = _conv_decode(state_conv[l], u3, wa, wb, cb, lg, lb)
        y_act = jnp.transpose(y_act, (1, 0, 2)).reshape(nb * t_new, d_conv)
        attn = _paged_attention(page_table, q, lat, kpe, cache_kv_latent, cache_k_rope, wts["g_k_full"],
                                wts["w_abs"], wts["w_ukt"], wts["w_uv"], layer=l, tk=512)
        ys = _out_ffn(ys, attn, y_act, gates, wts, tm=nb * t_new)
        outs[3].append(lat.reshape(nb, t_new, kv_lora))
        outs[4].append(kpe.reshape(nb, t_new, QK_ROPE))
        outs[5].append(jnp.concatenate([state_conv[l], u3], axis=1)[:, t_new:])

    return (yp.reshape(batch, seq, d_model), ys.reshape(nb, t_new, d_model),
            *(jnp.stack(o) for o in outs))
```

```python
import functools
import math

import jax
import jax.numpy as jnp
from jax import lax
from jax.experimental import pallas as pl
from jax.experimental.pallas import tpu as pltpu

N_HEADS = 8
QK_NOPE = 64
QK_ROPE = 32
ROPE_HALF = QK_ROPE // 2
QK_HEAD = QK_NOPE + QK_ROPE
V_HEAD = 64
ROPE_THETA = 10000.0
EPS = 1e-6
PAGE_SIZE = 128

LANES = 128
HEAD_TILE = LANES
NOPE_OFF = QK_ROPE
VMEM_LIMIT_BYTES = 56 * 1024 * 1024

BIAS_LANE = QK_HEAD
ONES_LANE = V_HEAD
LOG2E = 1.4426950408889634
FIXED_SHIFT_MAX_BOUND = 40.0

F32 = jnp.float32
BF16 = jnp.bfloat16
NEG_BIG = -1e30


def _dot(a, b):
    return jnp.dot(a, b, preferred_element_type=F32)


def _dot_nt(a, b):
    return lax.dot_general(a, b, (((1,), (1,)), ((), ())), preferred_element_type=F32)


def _rms_scale(x):
    return lax.rsqrt(jnp.mean(x * x, axis=-1, keepdims=True) + EPS)


def _const_spec(shape, single_buffer=False):
    nd = len(shape)
    if single_buffer:
        return pl.BlockSpec(shape, lambda *_: (0,) * nd, pipeline_mode=pl.Buffered(1))
    return pl.BlockSpec(shape, lambda *_: (0,) * nd)


def _rope_tile(t, c_tab, sl_tab, sr_tab):
    return t * c_tab + pltpu.roll(t, ROPE_HALF, 1) * sr_tab + pltpu.roll(t, LANES - ROPE_HALF, 1) * sl_tab


def _in_proj_kernel(x_ref, ctab_ref, sltab_ref, srtab_ref, gmix_ref, w_in_ref, gqa_ref, w_uq_ref,
                    gkva_ref, gq_ref, gk_ref, qone_ref, kbias_ref, vone_ref, w_uk_ref, w_uv_ref,
                    wabs_ref, sel_ref, u_ref, lat_ref, kpe_ref, gates_ref, *attn_refs,
                    d_conv, q_lora, kv_lora, d_model, decode):
    x = x_ref[...]
    hb = (x * _rms_scale(x) * gmix_ref[...]).astype(BF16)
    c_tab, sl_tab, sr_tab = ctab_ref[...], sltab_ref[...], srtab_ref[...]

    o_cq = 2 * d_conv
    o_ckv = o_cq + q_lora
    o_kpe = o_ckv + kv_lora
    o_gate = o_kpe + HEAD_TILE

    ag = _dot(hb, w_in_ref[:, 0:o_cq])
    u_ref[...] = ag[:, :d_conv] * jax.nn.sigmoid(ag[:, d_conv:])

    cq = _dot(hb, w_in_ref[:, o_cq:o_ckv])
    cqn = (cq * _rms_scale(cq) * gqa_ref[...]).astype(BF16)
    q = _dot(cqn, w_uq_ref[...])
    q_scale = LOG2E / math.sqrt(QK_HEAD)
    qpe = None
    for h in range(N_HEADS):
        sl = slice(h * HEAD_TILE, (h + 1) * HEAD_TILE)
        qh = _rope_tile(q[:, sl], c_tab, sl_tab, sr_tab)
        ss = jnp.sum(qh * qh, axis=-1, keepdims=True)
        qh = qh * lax.rsqrt(ss * (1.0 / QK_HEAD) + EPS) * (gq_ref[...] * q_scale)
        if decode:
            qa_ref, qpe_ref = attn_refs
            qg = (qh * gk_ref[...]).astype(BF16)
            qa_ref[:, h * kv_lora:(h + 1) * kv_lora] = _dot(qg, wabs_ref[h]).astype(qa_ref.dtype)
            part = _dot(qg, sel_ref[h])
            qpe = part if qpe is None else qpe + part
        else:
            attn_refs[0][:, sl] = (qh + qone_ref[...]).astype(attn_refs[0].dtype)
    if decode:
        qpe_ref[...] = qpe.astype(qpe_ref.dtype)

    ck = _dot(hb, w_in_ref[:, o_ckv:o_gate])
    ckv = ck[:, :kv_lora]
    lat = ckv * _rms_scale(ckv) * gkva_ref[...]
    lat_ref[...] = lat
    kpe = _rope_tile(ck[:, kv_lora:], c_tab, sl_tab, sr_tab)
    kpe_ref[...] = kpe[:, :QK_ROPE]

    for g in range(2):
        sl = slice(g * d_model, (g + 1) * d_model)
        gates_ref[:, sl] = jax.nn.sigmoid(_dot(hb, w_in_ref[:, o_gate + g * d_model:o_gate + (g + 1) * d_model]))

    if not decode:
        _, k_ref, v_ref = attn_refs
        latb = lat.astype(BF16)
        v_ref[...] = (_dot(latb, w_uv_ref[...]) + vone_ref[...]).astype(v_ref.dtype)
        kn = _dot(latb, w_uk_ref[...])
        for h in range(N_HEADS):
            sl = slice(h * HEAD_TILE, (h + 1) * HEAD_TILE)
            kh = kn[:, sl] + kpe
            ss = jnp.sum(kh * kh, axis=-1, keepdims=True)
            kh = kh * lax.rsqrt(ss * (1.0 / QK_HEAD) + EPS) * gk_ref[...] + kbias_ref[...]
            k_ref[:, sl] = kh.astype(k_ref.dtype)


def _in_proj(x, tabs, wts, *, tm, decode):
    t, d_model = x.shape
    n_tab_tiles = tabs[0].shape[0] // tm
    d_in_p = wts["w_in"].shape[1]
    q_lora = wts["w_uq"].shape[0]
    kv_lora = wts["w_uk"].shape[0]
    d_conv = (d_in_p - q_lora - kv_lora - HEAD_TILE - 2 * d_model) // 2
    qw = N_HEADS * HEAD_TILE
    vw = N_HEADS * HEAD_TILE

    row = lambda w: pl.BlockSpec((tm, w), lambda i: (i, 0))
    tab = pl.BlockSpec((tm, LANES), lambda i: (i % n_tab_tiles, 0))
    in_specs = [row(d_model), tab, tab, tab,
                _const_spec((1, d_model)), _const_spec(wts["w_in"].shape),
                _const_spec((1, q_lora)), _const_spec(wts["w_uq"].shape),
                _const_spec((1, kv_lora)), _const_spec((1, HEAD_TILE)), _const_spec((1, HEAD_TILE)),
                _const_spec((1, HEAD_TILE)), _const_spec((1, HEAD_TILE)), _const_spec((1, vw)),
                _const_spec(wts["w_uk"].shape), _const_spec(wts["w_uv_p"].shape),
                _const_spec(wts["w_abs"].shape), _const_spec(wts["w_sel"].shape)]
    out_shape = [jax.ShapeDtypeStruct((t, d_conv), F32), jax.ShapeDtypeStruct((t, kv_lora), F32),
                 jax.ShapeDtypeStruct((t, QK_ROPE), F32), jax.ShapeDtypeStruct((t, 2 * d_model), F32)]
    out_specs = [row(d_conv), row(kv_lora), row(QK_ROPE), row(2 * d_model)]
    if decode:
        widths = [N_HEADS * kv_lora, N_HEADS * QK_ROPE]
    else:
        widths = [qw, qw, vw]
    out_shape += [jax.ShapeDtypeStruct((t, w), BF16) for w in widths]
    out_specs += [row(w) for w in widths]
    kern = functools.partial(_in_proj_kernel, d_conv=d_conv, q_lora=q_lora, kv_lora=kv_lora,
                             d_model=d_model, decode=decode)
    return pl.pallas_call(
        kern, grid=(t // tm,), in_specs=in_specs, out_specs=out_specs, out_shape=out_shape,
        compiler_params=pltpu.CompilerParams(dimension_semantics=("arbitrary",),
                                             vmem_limit_bytes=VMEM_LIMIT_BYTES),
        name="in_proj_decode" if decode else "in_proj_prompt",
    )(x, *tabs, wts["g_mix"], wts["w_in"], wts["g_qa"], wts["w_uq"], wts["g_kva"], wts["g_q"], wts["g_k"],
      wts["q_one"], wts["k_bias"], wts["v_one"], wts["w_uk"], wts["w_uv_p"], wts["w_abs"], wts["w_sel"])


def _ln_silu(y, g, b):
    mu = jnp.mean(y, axis=-1, keepdims=True)
    yc = y - mu
    yn = yc * lax.rsqrt(jnp.mean(yc * yc, axis=-1, keepdims=True) + EPS) * g + b
    return yn * jax.nn.sigmoid(yn)


HALO = 32


def _conv_prompt_kernel(cur_ref, prev_ref, w_ref, b_ref, lng_ref, lnb_ref, y_ref, ext_ref, *, width, rows):
    ts = cur_ref.shape[1]
    ext_ref[HALO:, :] = cur_ref[0]
    prev = prev_ref[0]
    ext_ref[0:HALO, :] = jnp.where(pl.program_id(1) > 0, prev, jnp.zeros_like(prev))
    base = HALO - (width - 1)
    for r0 in range(0, ts, rows):
        acc = ext_ref[base + r0:base + r0 + rows, :] * w_ref[0:1, :]
        for k in range(1, width):
            acc = acc + ext_ref[base + r0 + k:base + r0 + k + rows, :] * w_ref[k:k + 1, :]
        y = _ln_silu(acc + b_ref[...], lng_ref[...], lnb_ref[...])
        y_ref[r0:r0 + rows, :] = y.astype(y_ref.dtype)


def _conv_prompt(u, conv_w, conv_b, ln_g, ln_b, *, batch, ts, rows=64):
    t, d_conv = u.shape
    seq = t // batch
    width = conv_w.shape[0]
    assert width - 1 <= HALO and ts % HALO == 0 and seq % ts == 0
    u3 = u.reshape(batch, seq, d_conv)
    nst = seq // ts
    kern = functools.partial(_conv_prompt_kernel, width=width, rows=rows)
    return pl.pallas_call(
        kern, grid=(batch, nst),
        in_specs=[pl.BlockSpec((1, ts, d_conv), lambda b, i: (b, i, 0)),
                  pl.BlockSpec((1, HALO, d_conv), lambda b, i: (b, jnp.maximum(i * (ts // HALO) - 1, 0), 0)),
                  _const_spec(conv_w.shape), _const_spec((1, d_conv)), _const_spec((1, d_conv)),
                  _const_spec((1, d_conv))],
        out_specs=pl.BlockSpec((ts, d_conv), lambda b, i: (b * nst + i, 0)),
        out_shape=jax.ShapeDtypeStruct((t, d_conv), BF16),
        scratch_shapes=[pltpu.VMEM((ts + HALO, d_conv), F32)],
        compiler_params=pltpu.CompilerParams(dimension_semantics=("arbitrary", "arbitrary"),
                                             vmem_limit_bytes=VMEM_LIMIT_BYTES),
        name="conv_prompt",
    )(u3, u3, conv_w, conv_b, ln_g, ln_b)


def _conv_decode_kernel(state_ref, unew_ref, w_ref, b_ref, lng_ref, lnb_ref, y_ref):
    n_state = state_ref.shape[0]
    for t in range(unew_ref.shape[0]):
        acc = None
        for k in range(w_ref.shape[0]):
            j = t + k
            src = state_ref[j] if j < n_state else unew_ref[j - n_state]
            term = src * w_ref[k:k + 1, :]
            acc = term if acc is None else acc + term
        y_ref[t] = _ln_silu(acc + b_ref[...], lng_ref[...], lnb_ref[...]).astype(y_ref.dtype)


def _conv_decode(state_t, u_new_t, conv_w, conv_b, ln_g, ln_b):
    t_new, nb, d_conv = u_new_t.shape
    assert state_t.shape[0] == conv_w.shape[0] - 1
    full = lambda a: _const_spec(a.shape)
    return pl.pallas_call(
        _conv_decode_kernel, grid=(1,),
        in_specs=[full(state_t), full(u_new_t), full(conv_w), _const_spec((1, d_conv)),
                  _const_spec((1, d_conv)), _const_spec((1, d_conv))],
        out_specs=_const_spec((t_new, nb, d_conv)),
        out_shape=jax.ShapeDtypeStruct((t_new, nb, d_conv), BF16),
        compiler_params=pltpu.CompilerParams(dimension_semantics=("arbitrary",),
                                             vmem_limit_bytes=VMEM_LIMIT_BYTES),
        name="conv_decode",
    )(state_t, u_new_t, conv_w, conv_b, ln_g, ln_b)


def _flash_kernel(qi_tab, ki_tab, q_ref, k_ref, v_ref, o_ref, acc_ref, *m_refs, online):
    step = pl.program_id(2)
    qi = qi_tab[step]
    ki = ki_tab[step]
    tq = q_ref.shape[1]
    tk = k_ref.shape[1]

    @pl.when(ki == 0)
    def _():
        acc_ref[...] = jnp.zeros_like(acc_ref)
        if online:
            m_refs[0][...] = jnp.full_like(m_refs[0], NEG_BIG)

    def accumulate(masked):
        if masked:
            visible = (lax.broadcasted_iota(jnp.int32, (tq, tk), 1)
                       <= lax.broadcasted_iota(jnp.int32, (tq, tk), 0))
        for hh in range(2):
            sl = slice(hh * HEAD_TILE, (hh + 1) * HEAD_TILE)
            s = _dot_nt(q_ref[0, :, sl], k_ref[0, :, sl])
            if masked:
                s = jnp.where(visible, s, NEG_BIG)
            if online:
                m_old = m_refs[0][hh]
                m_new = jnp.maximum(m_old, jnp.max(s, axis=-1, keepdims=True))
                m_refs[0][hh] = m_new
                p = jnp.exp2(s - m_new).astype(BF16)
                acc_ref[hh] = jnp.exp2(m_old - m_new) * acc_ref[hh] + _dot(p, v_ref[0, :, sl])
            else:
                acc_ref[hh] += _dot(jnp.exp2(s).astype(BF16), v_ref[0, :, sl])

    @pl.when(ki < qi)
    def _():
        accumulate(False)

    @pl.when(ki == qi)
    def _():
        accumulate(True)
        heads = []
        for hh in range(2):
            a = acc_ref[hh]
            heads.append(a / a[:, ONES_LANE:ONES_LANE + 1])
        lane = lax.broadcasted_iota(jnp.int32, heads[0].shape, 1)
        o_ref[0] = jnp.where(lane < V_HEAD, heads[0], pltpu.roll(heads[1], V_HEAD, 1)).astype(o_ref.dtype)


def _flash_attention(q, k, v, *, batch, tq, online):
    t = q.shape[0]
    seq = t // batch
    nq = seq // tq
    pairs = N_HEADS // 2
    assert 2 * V_HEAD == HEAD_TILE
    q3, k3, v3 = (a.reshape(batch, seq, a.shape[1]) for a in (q, k, v))
    qi_list = [qi for qi in range(nq) for _ in range(qi + 1)]
    ki_list = [ki for qi in range(nq) for ki in range(qi + 1)]
    qi_tab = jnp.asarray(qi_list, jnp.int32)
    ki_tab = jnp.asarray(ki_list, jnp.int32)
    pair_tile = lambda tab: pl.BlockSpec((1, tq, 2 * HEAD_TILE),
                                         lambda b, hp, s, qt, kt: (b, (qt if tab == "q" else kt)[s], hp))
    scratch = [pltpu.VMEM((2, tq, HEAD_TILE), F32)]
    if online:
        scratch.append(pltpu.VMEM((2, tq, 1), F32))
    grid_spec = pltpu.PrefetchScalarGridSpec(
        num_scalar_prefetch=2, grid=(batch, pairs, len(qi_list)),
        in_specs=[pair_tile("q"), pair_tile("k"), pair_tile("k")],
        out_specs=pl.BlockSpec((1, tq, 2 * V_HEAD), lambda b, hp, s, qt, kt: (b, qt[s], hp)),
        scratch_shapes=scratch)
    out = pl.pallas_call(
        functools.partial(_flash_kernel, online=online), grid_spec=grid_spec,
        out_shape=jax.ShapeDtypeStruct((batch, seq, N_HEADS * V_HEAD), BF16),
        compiler_params=pltpu.CompilerParams(dimension_semantics=("arbitrary",) * 3,
                                             vmem_limit_bytes=VMEM_LIMIT_BYTES),
        name="flash_prompt_online" if online else "flash_prompt",
    )(qi_tab, ki_tab, q3, k3, v3)
    return out.reshape(t, N_HEADS * V_HEAD)


CHUNK_UNROLL = 4


def _paged_kernel(pt_ref, qa_ref, qpe_ref, latn_ref, kpen_ref, wukt_ref, lat_hbm, kpet_hbm,
                  o_ref, lat_buf, kpet_buf, sems, cb_ref, cbn_ref, kpn_ref, s_ref, lhs_ref, *, layer, n_pages, tk):
    b = pl.program_id(0)
    nb = pl.num_programs(0)
    past = n_pages * PAGE_SIZE
    n_chunks = past // tk
    t_new = latn_ref.shape[1]
    rows = qa_ref.shape[1]

    def page_copies(seq, slot, j):
        page = pt_ref[seq * n_pages + j]
        dst = pl.ds(j * PAGE_SIZE, PAGE_SIZE)
        return (pltpu.make_async_copy(lat_hbm.at[layer, page], lat_buf.at[slot, dst], sems.at[slot, 0]),
                pltpu.make_async_copy(kpet_hbm.at[layer, page], kpet_buf.at[slot, j], sems.at[slot, 1]))

    def start_fetch(seq, slot):
        for j in range(n_pages):
            for cp in page_copies(seq, slot, j):
                cp.start()

    def wait_fetch(seq, slot):
        for j in range(n_pages):
            for cp in page_copies(seq, slot, j):
                cp.wait()

    slot = b % 2

    @pl.when(b == 0)
    def _():
        start_fetch(0, 0)
        lhs_ref[0:N_HEADS * QK_NOPE, :] = wukt_ref[...]
        cbn_ref[...] = jnp.zeros_like(cbn_ref)
        kpn_ref[...] = jnp.zeros_like(kpn_ref)

    @pl.when(b + 1 < nb)
    def _():
        start_fetch(b + 1, 1 - slot)

    lhs_ref[N_HEADS * QK_NOPE:, :] = qa_ref[0]
    qpe = qpe_ref[0]
    ones = jnp.ones((N_HEADS, QK_ROPE), BF16)
    pages_per_chunk = tk // PAGE_SIZE

    def scores(cb, ss_pe, s_pe):
        r = _dot_nt(lhs_ref[...], cb)
        n = cb.shape[0]
        kn = r[:N_HEADS * QK_NOPE]
        ss = jnp.sum((kn * kn).reshape(QK_NOPE, N_HEADS, n), axis=0) + ss_pe
        inv = lax.rsqrt(ss * (1.0 / QK_HEAD) + EPS)
        return (r[N_HEADS * QK_NOPE:] + s_pe) * jnp.concatenate([inv] * t_new, axis=0)

    wait_fetch(b, slot)

    def chunk(c, m):
        st = pl.multiple_of(c * tk, tk)
        cb = lat_buf[slot, pl.ds(st, tk), :].astype(BF16)
        cb_ref[pl.ds(st, tk), :] = cb
        kpt = jnp.concatenate([kpet_buf[slot, c * pages_per_chunk + i] for i in range(pages_per_chunk)],
                              axis=1)
        sc = scores(cb, jnp.sum(kpt * kpt, axis=0, keepdims=True), _dot(qpe, kpt.astype(BF16)))
        s_ref[c] = sc
        return jnp.maximum(m, sc)

    m = lax.fori_loop(0, n_chunks, chunk, jnp.full((rows, tk), NEG_BIG, F32), unroll=CHUNK_UNROLL)
    m = jnp.max(m, axis=-1, keepdims=True)

    cbn_ref[0:t_new, :] = latn_ref[0].astype(BF16)
    kpn_ref[0:t_new, :] = kpen_ref[0]
    cbn = cbn_ref[...]
    kpn = kpn_ref[...]
    sq = kpn * kpn
    hi = sq.astype(BF16)
    lo = (sq - hi.astype(F32)).astype(BF16)
    col = lax.broadcasted_iota(jnp.int32, (rows, LANES), 1)
    qrow = lax.broadcasted_iota(jnp.int32, (rows, LANES), 0) // N_HEADS
    s_new = scores(cbn, _dot_nt(ones, hi) + _dot_nt(ones, lo), _dot_nt(qpe, kpn.astype(BF16)))
    s_new = jnp.where(col <= qrow, s_new, NEG_BIG)
    m = jnp.maximum(m, jnp.max(s_new, axis=-1, keepdims=True))

    p_new = jnp.exp2(s_new - m)
    l0 = jnp.sum(p_new, axis=-1, keepdims=True)
    o0 = _dot(p_new.astype(BF16), cbn)

    def pv(c, carry):
        l, o = carry
        st = pl.multiple_of(c * tk, tk)
        p = jnp.exp2(s_ref[c] - m)
        return (l + p, o + _dot(p.astype(BF16), cb_ref[pl.ds(st, tk), :]))

    l, o = lax.fori_loop(0, n_chunks, pv, (jnp.zeros((rows, tk), F32), o0), unroll=CHUNK_UNROLL)
    l = l0 + jnp.sum(l, axis=-1, keepdims=True)
    o_ref[0] = (o / l).astype(o_ref.dtype)


def _paged_attention(page_table, qa, qpe, lat_new, kpe_new, cache_lat, cache_kpe_t, wukt, *, layer, tk):
    nb, n_pages = page_table.shape
    assert tk % PAGE_SIZE == 0 and cache_kpe_t.shape[2:] == (QK_ROPE, PAGE_SIZE)
    t_new = qa.shape[0] // nb
    kv_lora = cache_lat.shape[-1]
    past = n_pages * PAGE_SIZE
    rows = t_new * N_HEADS
    qa3 = qa.reshape(nb, rows, kv_lora)
    qpe3 = qpe.reshape(nb, rows, QK_ROPE)
    ln3 = lat_new.reshape(nb, t_new, kv_lora)
    kn3 = kpe_new.reshape(nb, t_new, QK_ROPE)
    per_seq = lambda r, w: pl.BlockSpec((1, r, w), lambda b, pt: (b, 0, 0))
    const = lambda a: pl.BlockSpec(a.shape, lambda b, pt: (0,) * a.ndim)
    grid_spec = pltpu.PrefetchScalarGridSpec(
        num_scalar_prefetch=1, grid=(nb,),
        in_specs=[per_seq(rows, kv_lora), per_seq(rows, QK_ROPE), per_seq(t_new, kv_lora),
                  per_seq(t_new, QK_ROPE), const(wukt), pl.BlockSpec(memory_space=pl.ANY),
                  pl.BlockSpec(memory_space=pl.ANY)],
        out_specs=per_seq(rows, kv_lora),
        scratch_shapes=[pltpu.VMEM((2, past, kv_lora), F32), pltpu.VMEM((2, n_pages, QK_ROPE, PAGE_SIZE), F32),
                        pltpu.SemaphoreType.DMA((2, 2)),
                        pltpu.VMEM((past, kv_lora), BF16), pltpu.VMEM((LANES, kv_lora), BF16),
                        pltpu.VMEM((LANES, QK_ROPE), F32), pltpu.VMEM((past // tk, rows, tk), F32),
                        pltpu.VMEM((N_HEADS * QK_NOPE + rows, kv_lora), BF16)])
    kern = functools.partial(_paged_kernel, layer=layer, n_pages=n_pages, tk=tk)
    out = pl.pallas_call(
        kern, grid_spec=grid_spec,
        out_shape=jax.ShapeDtypeStruct((nb, rows, kv_lora), BF16),
        compiler_params=pltpu.CompilerParams(dimension_semantics=("arbitrary",),
                                             vmem_limit_bytes=VMEM_LIMIT_BYTES),
        name="paged_decode",
    )(page_table.reshape(-1), qa3, qpe3, ln3, kn3, wukt, cache_lat, cache_kpe_t)
    return out.reshape(nb * t_new, N_HEADS * kv_lora)


FF_CHUNK = 256


def _out_ffn_kernel(x_ref, attn_ref, y_ref, gates_ref, wco_ref, womla_ref, wout_ref, gffn_ref,
                    wg_ref, wu_ref, wd_ref, *rest, latent_attn):
    d_model = x_ref.shape[1]
    d_ff = wg_ref.shape[1]
    conv_out = _dot(y_ref[...], wco_ref[...])
    if latent_attn:
        wuv_ref, o_ref, act_ref = rest
        attn = _dot(attn_ref[...], wuv_ref[...]).astype(BF16)
    else:
        o_ref, act_ref = rest
        attn = attn_ref[...]
    mla_out = _dot(attn, womla_ref[...])
    merged = gates_ref[:, :d_model] * conv_out + gates_ref[:, d_model:] * mla_out
    x1 = x_ref[...] + _dot(merged.astype(BF16), wout_ref[...])
    h2 = (x1 * _rms_scale(x1) * gffn_ref[...]).astype(BF16)
    for c0 in range(0, d_ff, FF_CHUNK):
        sl = slice(c0, c0 + FF_CHUNK)
        g = _dot(h2, wg_ref[:, sl])
        act_ref[:, sl] = (g * jax.nn.sigmoid(g) * _dot(h2, wu_ref[:, sl])).astype(BF16)
    o_ref[...] = x1 + _dot(act_ref[...], wd_ref[...])


def _out_ffn(x, attn, y_act, gates, wts, *, tm, latent_attn):
    t, d_model = x.shape
    d_ff = wts["w_gate"].shape[1]
    assert d_ff % FF_CHUNK == 0
    row = lambda w: pl.BlockSpec((tm, w), lambda i: (i, 0))
    wspec = lambda a: _const_spec(a.shape, single_buffer=True)
    weights = [wts["w_conv_out"], wts["w_o_mla"], wts["w_out"], wts["g_ffn"], wts["w_gate"], wts["w_up"],
               wts["w_down"]] + ([wts["w_uv_bd"]] if latent_attn else [])
    return pl.pallas_call(
        functools.partial(_out_ffn_kernel, latent_attn=latent_attn), grid=(t // tm,),
        in_specs=[row(d_model), row(attn.shape[1]), row(y_act.shape[1]), row(2 * d_model)]
                 + [wspec(w) for w in weights],
        out_specs=row(d_model),
        out_shape=jax.ShapeDtypeStruct((t, d_model), F32),
        scratch_shapes=[pltpu.VMEM((tm, d_ff), BF16)],
        compiler_params=pltpu.CompilerParams(dimension_semantics=("arbitrary",),
                                             vmem_limit_bytes=VMEM_LIMIT_BYTES),
        name="out_ffn_decode" if latent_attn else "out_ffn_prompt",
    )(x, attn, y_act, gates, *weights)


def _head_tile_cols(w_rope, w_nope):
    lead = w_nope.shape[:-2]
    pad = jnp.zeros(lead + (N_HEADS, HEAD_TILE - QK_HEAD), w_nope.dtype)
    tile = jnp.concatenate([w_rope, w_nope, pad], axis=-1)
    return tile.reshape(lead + (N_HEADS * HEAD_TILE,))


def _gain_tile(g_compact):
    g = g_compact.astype(F32)
    return jnp.concatenate([g[QK_NOPE:], g[QK_NOPE:], g[:QK_NOPE], jnp.zeros((HEAD_TILE - QK_HEAD,), F32)])[None]


def _rope_tables(pos):
    inv_freq = 1.0 / (ROPE_THETA ** (jnp.arange(0, QK_ROPE, 2, dtype=F32) / QK_ROPE))
    ang = pos.astype(F32)[:, None] * inv_freq[None, :]
    c, s = jnp.cos(ang), jnp.sin(ang)
    n = pos.shape[0]
    z16 = jnp.zeros((n, ROPE_HALF), F32)
    rest = LANES - QK_ROPE
    c_tab = jnp.concatenate([c, c, jnp.ones((n, QK_NOPE), F32), jnp.zeros((n, rest - QK_NOPE), F32)], axis=1)
    sl_tab = jnp.concatenate([-s, z16, jnp.zeros((n, rest), F32)], axis=1)
    sr_tab = jnp.concatenate([z16, s, jnp.zeros((n, rest), F32)], axis=1)
    return c_tab, sl_tab, sr_tab


def _layer_weights(l, norm_mix_g, w_in, q_a_norm_g, w_uq, kv_a_norm_g, w_uk, w_uv, q_norm_g, k_norm_g,
                   w_o_mla, w_conv_out, w_out, norm_ffn_g, w_gate, w_up, w_down):
    d_model = w_in.shape[1]
    kv_lora = w_uk.shape[1]
    q_lora = w_uq.shape[1]
    d_conv = w_conv_out.shape[1]
    o1 = 2 * d_conv
    o2 = o1 + q_lora
    o3 = o2 + kv_lora
    o4 = o3 + QK_ROPE
    wi = w_in[l]
    w_in_p = jnp.concatenate([wi[:, :o3], wi[:, o3:o4], jnp.zeros((d_model, HEAD_TILE - QK_ROPE), wi.dtype),
                              wi[:, o4:]], axis=1).astype(BF16)
    uq = w_uq[l].reshape(q_lora, N_HEADS, QK_HEAD)
    w_uq_p = _head_tile_cols(uq[..., QK_NOPE:], uq[..., :QK_NOPE]).astype(BF16)
    uk = w_uk[l]
    w_uk_p = _head_tile_cols(jnp.zeros((kv_lora, N_HEADS, QK_ROPE), uk.dtype), uk).astype(BF16)
    w_ukt = jnp.transpose(uk, (2, 1, 0)).reshape(QK_NOPE * N_HEADS, kv_lora).astype(BF16)
    uv = w_uv[l]
    w_uv_p = jnp.concatenate([uv, jnp.zeros((kv_lora, N_HEADS, HEAD_TILE - V_HEAD), uv.dtype)],
                             axis=-1).reshape(kv_lora, N_HEADS * HEAD_TILE).astype(BF16)
    lane = jnp.arange(HEAD_TILE)
    bound = (1.02 * LOG2E * math.sqrt(QK_HEAD)) * jnp.max(jnp.abs(q_norm_g[l])) * jnp.max(jnp.abs(k_norm_g[l]))
    q_one = jnp.where(lane == BIAS_LANE, 1.0, 0.0).astype(F32)[None]
    k_bias = jnp.where(lane == BIAS_LANE, -bound, 0.0).astype(F32)[None]
    v_one = jnp.tile(jnp.where(lane == ONES_LANE, 1.0, 0.0).astype(F32), N_HEADS)[None]
    ukt = jnp.transpose(uk, (1, 2, 0))
    w_abs = jnp.concatenate([jnp.zeros((N_HEADS, QK_ROPE, kv_lora), uk.dtype), ukt,
                             jnp.zeros((N_HEADS, HEAD_TILE - QK_HEAD, kv_lora), uk.dtype)], axis=1).astype(BF16)
    lane_ids = jnp.arange(HEAD_TILE)[None, :, None]
    col_ids = jnp.arange(N_HEADS * QK_ROPE)[None, None, :]
    head_ids = jnp.arange(N_HEADS)[:, None, None]
    w_sel = ((lane_ids < QK_ROPE) & (col_ids == head_ids * QK_ROPE + lane_ids)).astype(BF16)
    eye_h = jnp.eye(N_HEADS, dtype=uv.dtype)
    w_uv_bd = jnp.einsum("chd,hg->hcgd", uv, eye_h).reshape(N_HEADS * kv_lora, N_HEADS * V_HEAD).astype(BF16)
    g_k = _gain_tile(k_norm_g[l])
    return dict(
        g_mix=norm_mix_g[l][None].astype(F32), w_in=w_in_p, g_qa=q_a_norm_g[l][None].astype(F32), w_uq=w_uq_p,
        g_kva=kv_a_norm_g[l][None].astype(F32), g_q=_gain_tile(q_norm_g[l]), g_k=g_k,
        w_uk=w_uk_p, w_uv_p=w_uv_p, w_uv_bd=w_uv_bd, q_one=q_one, k_bias=k_bias, v_one=v_one, score_bound=bound,
        w_ukt=w_ukt, w_abs=w_abs, w_sel=w_sel, w_conv_out=w_conv_out[l].astype(BF16),
        w_o_mla=w_o_mla[l].astype(BF16),
        w_out=w_out[l].astype(BF16), g_ffn=norm_ffn_g[l][None].astype(F32), w_gate=w_gate[l].astype(BF16),
        w_up=w_up[l].astype(BF16), w_down=w_down[l].astype(BF16))


def kernel(x_prompt, x_sample, cache_kv_latent, cache_k_rope, state_conv, page_table, norm_mix_g, w_in,
           q_a_norm_g, w_uq, kv_a_norm_g, w_uk, w_uv, q_norm_g, k_norm_g, w_o_mla, conv_w, conv_b, conv_ln_g,
           conv_ln_b, w_conv_out, w_out, norm_ffn_g, w_gate, w_up, w_down):
    batch, seq, d_model = x_prompt.shape
    nb, t_new, _ = x_sample.shape
    depth = w_in.shape[0]
    n_pages = page_table.shape[1]
    past = n_pages * PAGE_SIZE
    n_state = state_conv.shape[2]
    d_conv = state_conv.shape[3]
    kv_lora = cache_kv_latent.shape[-1]

    tabs_p = _rope_tables(jnp.arange(seq))
    tabs_s = _rope_tables(jnp.tile(past + jnp.arange(t_new), nb))

    yp = x_prompt.reshape(batch * seq, d_model)
    ys = x_sample.reshape(nb * t_new, d_model)
    outs = [[] for _ in range(6)]
    for l in range(depth):
        wts = _layer_weights(l, norm_mix_g, w_in, q_a_norm_g, w_uq, kv_a_norm_g, w_uk, w_uv, q_norm_g,
                             k_norm_g, w_o_mla, w_conv_out, w_out, norm_ffn_g, w_gate, w_up, w_down)
        cb, lg, lb = (a[l][None].astype(F32) for a in (conv_b, conv_ln_g, conv_ln_b))

        u, lat, kpe, gates, q, k, v = _in_proj(yp, tabs_p, wts, tm=512, decode=False)
        y_act = _conv_prompt(u, conv_w[l].astype(F32), cb, lg, lb, batch=batch, ts=512)
        attn = lax.cond(wts["score_bound"] <= FIXED_SHIFT_MAX_BOUND,
                        functools.partial(_flash_attention, batch=batch, tq=512, online=False),
                        functools.partial(_flash_attention, batch=batch, tq=512, online=True), q, k, v)
        yp = _out_ffn(yp, attn, y_act, gates, wts, tm=512, latent_attn=False)
        outs[0].append(lat.reshape(batch, seq, kv_lora))
        outs[1].append(kpe.reshape(batch, seq, QK_ROPE))
        outs[2].append(u.reshape(batch, seq, d_conv)[:, seq - n_state:])

        u, lat, kpe, gates, qa, qpe = _in_proj(ys, tabs_s, wts, tm=nb * t_new, decode=True)
        u_t = jnp.transpose(u.reshape(nb, t_new, d_conv), (1, 0, 2))
        state_t = jnp.transpose(state_conv[l], (1, 0, 2))
        y_act = _conv_decode(state_t, u_t, conv_w[l].astype(F32), cb, lg, lb)
        y_act = jnp.transpose(y_act, (1, 0, 2)).reshape(nb * t_new, d_conv)
        attn = _paged_attention(page_table, qa, qpe, lat, kpe, cache_kv_latent, jnp.swapaxes(cache_k_rope, 2, 3),
                                wts["w_ukt"], layer=l, tk=512)
        ys = _out_ffn(ys, attn, y_act, gates, wts, tm=nb * t_new, latent_attn=True)
        outs[3].append(lat.reshape(nb, t_new, kv_lora))
        outs[4].append(kpe.reshape(nb, t_new, QK_ROPE))
        outs[5].append(jnp.transpose(jnp.concatenate([state_t[t_new:], u_t], axis=0), (1, 0, 2)))

    return (yp.reshape(batch, seq, d_model), ys.reshape(nb, t_new, d_model),
            *(jnp.stack(o) for o in outs))
```

```python
import functools
import math

import jax
import jax.numpy as jnp
from jax import lax
from jax.experimental import pallas as pl
from jax.experimental.pallas import tpu as pltpu

N_HEADS = 8
QK_NOPE = 64
QK_ROPE = 32
ROPE_HALF = QK_ROPE // 2
QK_HEAD = QK_NOPE + QK_ROPE
V_HEAD = 64
ROPE_THETA = 10000.0
EPS = 1e-6
PAGE_SIZE = 128

LANES = 128
HEAD_TILE = LANES
NOPE_OFF = QK_ROPE
VMEM_LIMIT_BYTES = 56 * 1024 * 1024

BIAS_LANE = QK_HEAD
ONES_LANE = V_HEAD
LOG2E = 1.4426950408889634
FIXED_SHIFT_MAX_BOUND = 40.0

F32 = jnp.float32
BF16 = jnp.bfloat16
NEG_BIG = -1e30


def _dot(a, b):
    return jnp.dot(a, b, preferred_element_type=F32)


def _dot_nt(a, b):
    return lax.dot_general(a, b, (((1,), (1,)), ((), ())), preferred_element_type=F32)


def _rms_scale(x):
    return lax.rsqrt(jnp.mean(x * x, axis=-1, keepdims=True) + EPS)


def _const_spec(shape, single_buffer=False):
    nd = len(shape)
    if single_buffer:
        return pl.BlockSpec(shape, lambda *_: (0,) * nd, pipeline_mode=pl.Buffered(1))
    return pl.BlockSpec(shape, lambda *_: (0,) * nd)


def _rope_tile(t, c_tab, sl_tab, sr_tab):
    return t * c_tab + pltpu.roll(t, ROPE_HALF, 1) * sr_tab + pltpu.roll(t, LANES - ROPE_HALF, 1) * sl_tab


def _in_proj_kernel(x_ref, ctab_ref, sltab_ref, srtab_ref, gmix_ref, w_in_ref, gqa_ref, w_uq_ref,
                    gkva_ref, gq_ref, gk_ref, qone_ref, kbias_ref, vone_ref, w_uk_ref, w_uv_ref,
                    wabs_ref, sel_ref, u_ref, lat_ref, kpe_ref, gates_ref, *attn_refs,
                    d_conv, q_lora, kv_lora, d_model, decode):
    x = x_ref[...]
    hb = (x * _rms_scale(x) * gmix_ref[...]).astype(BF16)
    c_tab, sl_tab, sr_tab = ctab_ref[...], sltab_ref[...], srtab_ref[...]

    o_cq = 2 * d_conv
    o_ckv = o_cq + q_lora
    o_kpe = o_ckv + kv_lora
    o_gate = o_kpe + HEAD_TILE

    ag = _dot(hb, w_in_ref[:, 0:o_cq])
    u_ref[...] = ag[:, :d_conv] * jax.nn.sigmoid(ag[:, d_conv:])

    cq = _dot(hb, w_in_ref[:, o_cq:o_ckv])
    cqn = (cq * _rms_scale(cq) * gqa_ref[...]).astype(BF16)
    q = _dot(cqn, w_uq_ref[...])
    q_scale = LOG2E / math.sqrt(QK_HEAD)
    qpe = None
    for h in range(N_HEADS):
        sl = slice(h * HEAD_TILE, (h + 1) * HEAD_TILE)
        qh = _rope_tile(q[:, sl], c_tab, sl_tab, sr_tab)
        ss = jnp.sum(qh * qh, axis=-1, keepdims=True)
        qh = qh * lax.rsqrt(ss * (1.0 / QK_HEAD) + EPS) * (gq_ref[...] * q_scale)
        if decode:
            qa_ref, qpe_ref = attn_refs
            qg = (qh * gk_ref[...]).astype(BF16)
            qa_ref[:, h * kv_lora:(h + 1) * kv_lora] = _dot(qg, wabs_ref[h]).astype(qa_ref.dtype)
            part = _dot(qg, sel_ref[h])
            qpe = part if qpe is None else qpe + part
        else:
            attn_refs[0][:, sl] = (qh + qone_ref[...]).astype(attn_refs[0].dtype)
    if decode:
        qpe_ref[...] = qpe.astype(qpe_ref.dtype)

    ck = _dot(hb, w_in_ref[:, o_ckv:o_gate])
    ckv = ck[:, :kv_lora]
    lat = ckv * _rms_scale(ckv) * gkva_ref[...]
    lat_ref[...] = lat
    kpe = _rope_tile(ck[:, kv_lora:], c_tab, sl_tab, sr_tab)
    kpe_ref[...] = kpe[:, :QK_ROPE]

    for g in range(2):
        sl = slice(g * d_model, (g + 1) * d_model)
        gates_ref[:, sl] = jax.nn.sigmoid(_dot(hb, w_in_ref[:, o_gate + g * d_model:o_gate + (g + 1) * d_model]))

    if not decode:
        _, k_ref, v_ref = attn_refs
        latb = lat.astype(BF16)
        v_ref[...] = (_dot(latb, w_uv_ref[...]) + vone_ref[...]).astype(v_ref.dtype)
        kn = _dot(latb, w_uk_ref[...])
        for h in range(N_HEADS):
            sl = slice(h * HEAD_TILE, (h + 1) * HEAD_TILE)
            kh = kn[:, sl] + kpe
            ss = jnp.sum(kh * kh, axis=-1, keepdims=True)
            kh = kh * lax.rsqrt(ss * (1.0 / QK_HEAD) + EPS) * gk_ref[...] + kbias_ref[...]
            k_ref[:, sl] = kh.astype(k_ref.dtype)


def _in_proj(x, tabs, wts, *, tm, decode):
    t, d_model = x.shape
    n_tab_tiles = tabs[0].shape[0] // tm
    d_in_p = wts["w_in"].shape[1]
    q_lora = wts["w_uq"].shape[0]
    kv_lora = wts["w_uk"].shape[0]
    d_conv = (d_in_p - q_lora - kv_lora - HEAD_TILE - 2 * d_model) // 2
    qw = N_HEADS * HEAD_TILE
    vw = N_HEADS * HEAD_TILE

    row = lambda w: pl.BlockSpec((tm, w), lambda i: (i, 0))
    tab = pl.BlockSpec((tm, LANES), lambda i: (i % n_tab_tiles, 0))
    in_specs = [row(d_model), tab, tab, tab,
                _const_spec((1, d_model)), _const_spec(wts["w_in"].shape),
                _const_spec((1, q_lora)), _const_spec(wts["w_uq"].shape),
                _const_spec((1, kv_lora)), _const_spec((1, HEAD_TILE)), _const_spec((1, HEAD_TILE)),
                _const_spec((1, HEAD_TILE)), _const_spec((1, HEAD_TILE)), _const_spec((1, vw)),
                _const_spec(wts["w_uk"].shape), _const_spec(wts["w_uv_p"].shape),
                _const_spec(wts["w_abs"].shape), _const_spec(wts["w_sel"].shape)]
    out_shape = [jax.ShapeDtypeStruct((t, d_conv), F32), jax.ShapeDtypeStruct((t, kv_lora), F32),
                 jax.ShapeDtypeStruct((t, QK_ROPE), F32), jax.ShapeDtypeStruct((t, 2 * d_model), F32)]
    out_specs = [row(d_conv), row(kv_lora), row(QK_ROPE), row(2 * d_model)]
    if decode:
        widths = [N_HEADS * kv_lora, N_HEADS * QK_ROPE]
    else:
        widths = [qw, qw, vw]
    out_shape += [jax.ShapeDtypeStruct((t, w), BF16) for w in widths]
    out_specs += [row(w) for w in widths]
    kern = functools.partial(_in_proj_kernel, d_conv=d_conv, q_lora=q_lora, kv_lora=kv_lora,
                             d_model=d_model, decode=decode)
    return pl.pallas_call(
        kern, grid=(t // tm,), in_specs=in_specs, out_specs=out_specs, out_shape=out_shape,
        compiler_params=pltpu.CompilerParams(dimension_semantics=("arbitrary",),
                                             vmem_limit_bytes=VMEM_LIMIT_BYTES),
        name="in_proj_decode" if decode else "in_proj_prompt",
    )(x, *tabs, wts["g_mix"], wts["w_in"], wts["g_qa"], wts["w_uq"], wts["g_kva"], wts["g_q"], wts["g_k"],
      wts["q_one"], wts["k_bias"], wts["v_one"], wts["w_uk"], wts["w_uv_p"], wts["w_abs"], wts["w_sel"])


def _ln_silu(y, g, b):
    mu = jnp.mean(y, axis=-1, keepdims=True)
    yc = y - mu
    yn = yc * lax.rsqrt(jnp.mean(yc * yc, axis=-1, keepdims=True) + EPS) * g + b
    return yn * jax.nn.sigmoid(yn)


HALO = 32


def _conv_prompt_kernel(cur_ref, prev_ref, w_ref, b_ref, lng_ref, lnb_ref, y_ref, ext_ref, *, width, rows):
    ts = cur_ref.shape[1]
    ext_ref[HALO:, :] = cur_ref[0]
    prev = prev_ref[0]
    ext_ref[0:HALO, :] = jnp.where(pl.program_id(1) > 0, prev, jnp.zeros_like(prev))
    base = HALO - (width - 1)
    for r0 in range(0, ts, rows):
        acc = ext_ref[base + r0:base + r0 + rows, :] * w_ref[0:1, :]
        for k in range(1, width):
            acc = acc + ext_ref[base + r0 + k:base + r0 + k + rows, :] * w_ref[k:k + 1, :]
        y = _ln_silu(acc + b_ref[...], lng_ref[...], lnb_ref[...])
        y_ref[r0:r0 + rows, :] = y.astype(y_ref.dtype)


def _conv_prompt(u, conv_w, conv_b, ln_g, ln_b, *, batch, ts, rows=64):
    t, d_conv = u.shape
    seq = t // batch
    width = conv_w.shape[0]
    assert width - 1 <= HALO and ts % HALO == 0 and seq % ts == 0
    u3 = u.reshape(batch, seq, d_conv)
    nst = seq // ts
    kern = functools.partial(_conv_prompt_kernel, width=width, rows=rows)
    return pl.pallas_call(
        kern, grid=(batch, nst),
        in_specs=[pl.BlockSpec((1, ts, d_conv), lambda b, i: (b, i, 0)),
                  pl.BlockSpec((1, HALO, d_conv), lambda b, i: (b, jnp.maximum(i * (ts // HALO) - 1, 0), 0)),
                  _const_spec(conv_w.shape), _const_spec((1, d_conv)), _const_spec((1, d_conv)),
                  _const_spec((1, d_conv))],
        out_specs=pl.BlockSpec((ts, d_conv), lambda b, i: (b * nst + i, 0)),
        out_shape=jax.ShapeDtypeStruct((t, d_conv), BF16),
        scratch_shapes=[pltpu.VMEM((ts + HALO, d_conv), F32)],
        compiler_params=pltpu.CompilerParams(dimension_semantics=("arbitrary", "arbitrary"),
                                             vmem_limit_bytes=VMEM_LIMIT_BYTES),
        name="conv_prompt",
    )(u3, u3, conv_w, conv_b, ln_g, ln_b)


def _conv_decode_kernel(state_ref, unew_ref, w_ref, b_ref, lng_ref, lnb_ref, y_ref):
    n_state = state_ref.shape[0]
    for t in range(unew_ref.shape[0]):
        acc = None
        for k in range(w_ref.shape[0]):
            j = t + k
            src = state_ref[j] if j < n_state else unew_ref[j - n_state]
            term = src * w_ref[k:k + 1, :]
            acc = term if acc is None else acc + term
        y_ref[t] = _ln_silu(acc + b_ref[...], lng_ref[...], lnb_ref[...]).astype(y_ref.dtype)


def _conv_decode(state_t, u_new_t, conv_w, conv_b, ln_g, ln_b):
    t_new, nb, d_conv = u_new_t.shape
    assert state_t.shape[0] == conv_w.shape[0] - 1
    full = lambda a: _const_spec(a.shape)
    return pl.pallas_call(
        _conv_decode_kernel, grid=(1,),
        in_specs=[full(state_t), full(u_new_t), full(conv_w), _const_spec((1, d_conv)),
                  _const_spec((1, d_conv)), _const_spec((1, d_conv))],
        out_specs=_const_spec((t_new, nb, d_conv)),
        out_shape=jax.ShapeDtypeStruct((t_new, nb, d_conv), BF16),
        compiler_params=pltpu.CompilerParams(dimension_semantics=("arbitrary",),
                                             vmem_limit_bytes=VMEM_LIMIT_BYTES),
        name="conv_decode",
    )(state_t, u_new_t, conv_w, conv_b, ln_g, ln_b)


def _flash_kernel(qi_tab, ki_tab, q_ref, k_ref, v_ref, o_ref, acc_ref, *m_refs, online):
    step = pl.program_id(2)
    qi = qi_tab[step]
    ki = ki_tab[step]
    tq = q_ref.shape[1]
    tk = k_ref.shape[1]
    n_heads = acc_ref.shape[0]

    @pl.when(ki == 0)
    def _():
        acc_ref[...] = jnp.zeros_like(acc_ref)
        if online:
            m_refs[0][...] = jnp.full_like(m_refs[0], NEG_BIG)

    def accumulate(masked):
        if masked:
            visible = (lax.broadcasted_iota(jnp.int32, (tq, tk), 1)
                       <= lax.broadcasted_iota(jnp.int32, (tq, tk), 0))
        for hh in range(n_heads):
            sl = slice(hh * HEAD_TILE, (hh + 1) * HEAD_TILE)
            s = _dot_nt(q_ref[0, :, sl], k_ref[0, :, sl])
            if masked:
                s = jnp.where(visible, s, NEG_BIG)
            if online:
                m_old = m_refs[0][hh]
                m_new = jnp.maximum(m_old, jnp.max(s, axis=-1, keepdims=True))
                m_refs[0][hh] = m_new
                p = jnp.exp2(s - m_new).astype(BF16)
                acc_ref[hh] = jnp.exp2(m_old - m_new) * acc_ref[hh] + _dot(p, v_ref[0, :, sl])
            else:
                acc_ref[hh] += _dot(jnp.exp2(s).astype(BF16), v_ref[0, :, sl])

    @pl.when(ki < qi)
    def _():
        accumulate(False)

    @pl.when(ki == qi)
    def _():
        accumulate(True)
        lane = lax.broadcasted_iota(jnp.int32, (tq, HEAD_TILE), 1)
        for pair in range(n_heads // 2):
            a0, a1 = acc_ref[2 * pair], acc_ref[2 * pair + 1]
            h0 = a0 / a0[:, ONES_LANE:ONES_LANE + 1]
            h1 = a1 / a1[:, ONES_LANE:ONES_LANE + 1]
            o_ref[0, :, pair * HEAD_TILE:(pair + 1) * HEAD_TILE] = jnp.where(
                lane < V_HEAD, h0, pltpu.roll(h1, V_HEAD, 1)).astype(o_ref.dtype)


def _flash_attention(q, k, v, *, batch, tq, heads_per_step, online):
    t = q.shape[0]
    seq = t // batch
    nq = seq // tq
    assert 2 * V_HEAD == HEAD_TILE and heads_per_step % 2 == 0 and N_HEADS % heads_per_step == 0
    q3, k3, v3 = (a.reshape(batch, seq, a.shape[1]) for a in (q, k, v))
    qi_list = [qi for qi in range(nq) for _ in range(qi + 1)]
    ki_list = [ki for qi in range(nq) for ki in range(qi + 1)]
    qi_tab = jnp.asarray(qi_list, jnp.int32)
    ki_tab = jnp.asarray(ki_list, jnp.int32)
    pair_tile = lambda tab: pl.BlockSpec((1, tq, heads_per_step * HEAD_TILE),
                                         lambda b, hp, s, qt, kt: (b, (qt if tab == "q" else kt)[s], hp))
    scratch = [pltpu.VMEM((heads_per_step, tq, HEAD_TILE), F32)]
    if online:
        scratch.append(pltpu.VMEM((heads_per_step, tq, 1), F32))
    grid_spec = pltpu.PrefetchScalarGridSpec(
        num_scalar_prefetch=2, grid=(batch, N_HEADS // heads_per_step, len(qi_list)),
        in_specs=[pair_tile("q"), pair_tile("k"), pair_tile("k")],
        out_specs=pl.BlockSpec((1, tq, heads_per_step * V_HEAD), lambda b, hp, s, qt, kt: (b, qt[s], hp)),
        scratch_shapes=scratch)
    out = pl.pallas_call(
        functools.partial(_flash_kernel, online=online), grid_spec=grid_spec,
        out_shape=jax.ShapeDtypeStruct((batch, seq, N_HEADS * V_HEAD), BF16),
        compiler_params=pltpu.CompilerParams(dimension_semantics=("arbitrary",) * 3,
                                             vmem_limit_bytes=VMEM_LIMIT_BYTES),
        name="flash_prompt_online" if online else "flash_prompt",
    )(qi_tab, ki_tab, q3, k3, v3)
    return out.reshape(t, N_HEADS * V_HEAD)


CHUNK_UNROLL = 2


def _paged_kernel(pt_ref, qa_ref, qpe_ref, latn_ref, kpen_ref, wukt_ref, lat_hbm, kpet_hbm,
                  o_ref, lat_buf, kpet_buf, sems, cb_ref, cbt_ref, cbn_ref, kpn_ref, s_ref, lhs_ref, *,
                  layer, n_pages, tk):
    b = pl.program_id(0)
    nb = pl.num_programs(0)
    past = n_pages * PAGE_SIZE
    n_chunks = past // tk
    t_new = latn_ref.shape[1]
    rows = qa_ref.shape[1]

    def page_copies(seq, slot, j):
        page = pt_ref[seq * n_pages + j]
        dst = pl.ds(j * PAGE_SIZE, PAGE_SIZE)
        return (pltpu.make_async_copy(lat_hbm.at[layer, page], lat_buf.at[slot, dst], sems.at[slot, 0]),
                pltpu.make_async_copy(kpet_hbm.at[layer, page], kpet_buf.at[slot, j], sems.at[slot, 1]))

    def start_fetch(seq, slot):
        for j in range(n_pages):
            for cp in page_copies(seq, slot, j):
                cp.start()

    def wait_fetch(seq, slot):
        for j in range(n_pages):
            for cp in page_copies(seq, slot, j):
                cp.wait()

    slot = b % 2

    @pl.when(b == 0)
    def _():
        start_fetch(0, 0)
        lhs_ref[0:N_HEADS * QK_NOPE, :] = wukt_ref[...]
        cbn_ref[...] = jnp.zeros_like(cbn_ref)
        kpn_ref[...] = jnp.zeros_like(kpn_ref)

    @pl.when(b + 1 < nb)
    def _():
        start_fetch(b + 1, 1 - slot)

    lhs_ref[N_HEADS * QK_NOPE:, :] = qa_ref[0]
    qpe = qpe_ref[0]
    ones = jnp.ones((N_HEADS, QK_ROPE), BF16)
    pages_per_chunk = tk // PAGE_SIZE

    def scores(cbt, ss_pe, s_pe):
        r = _dot(lhs_ref[...], cbt)
        n = cbt.shape[1]
        kn = r[:N_HEADS * QK_NOPE]
        ss = jnp.sum((kn * kn).reshape(QK_NOPE, N_HEADS, n), axis=0) + ss_pe
        inv = lax.rsqrt(ss * (1.0 / QK_HEAD) + EPS)
        return (r[N_HEADS * QK_NOPE:] + s_pe) * jnp.concatenate([inv] * t_new, axis=0)

    wait_fetch(b, slot)

    def lane_tiles(a):
        return [a[:, i:i + LANES] for i in range(0, a.shape[1], LANES)]

    def chunk_group(g, m):
        for j in range(CHUNK_UNROLL):
            c = g * CHUNK_UNROLL + j
            st = pl.multiple_of(c * tk, tk)
            cb = lat_buf[slot, pl.ds(st, tk), :].astype(BF16)
            cb_ref[pl.ds(st, tk), :] = cb
            kpt = jnp.concatenate([kpet_buf[slot, c * pages_per_chunk + i] for i in range(pages_per_chunk)],
                                  axis=1)
            cbt_ref[j] = cb.T
            sc = scores(cbt_ref[j], jnp.sum(kpt * kpt, axis=0, keepdims=True), _dot(qpe, kpt.astype(BF16)))
            s_ref[c] = sc
            m = functools.reduce(jnp.maximum, [m] + lane_tiles(sc))
        return m

    m = lax.fori_loop(0, n_chunks // CHUNK_UNROLL, chunk_group, jnp.full((rows, LANES), NEG_BIG, F32))
    m = jnp.max(m, axis=-1, keepdims=True)

    cbn_ref[0:t_new, :] = latn_ref[0].astype(BF16)
    kpn_ref[0:t_new, :] = kpen_ref[0]
    cbn = cbn_ref[...]
    kpn = kpn_ref[...]
    sq = kpn * kpn
    hi = sq.astype(BF16)
    lo = (sq - hi.astype(F32)).astype(BF16)
    col = lax.broadcasted_iota(jnp.int32, (rows, LANES), 1)
    qrow = lax.broadcasted_iota(jnp.int32, (rows, LANES), 0) // N_HEADS
    s_new = scores(cbn.astype(F32).T.astype(BF16), _dot_nt(ones, hi) + _dot_nt(ones, lo),
                   _dot_nt(qpe, kpn.astype(BF16)))
    s_new = jnp.where(col <= qrow, s_new, NEG_BIG)
    m_new = jnp.max(s_new, axis=-1, keepdims=True)
    p_new = jnp.exp2(s_new - m_new)
    l_new = jnp.sum(p_new, axis=-1, keepdims=True)
    o_new = _dot(p_new.astype(BF16), cbn)

    l = jnp.zeros((rows, LANES), F32)
    o = jnp.zeros((rows, cb_ref.shape[1]), F32)
    for c in range(n_chunks):
        p = jnp.exp2(s_ref[c] - m)
        l = functools.reduce(jnp.add, [l] + lane_tiles(p))
        o = o + _dot(p.astype(BF16), cb_ref[c * tk:(c + 1) * tk, :])
    l = jnp.sum(l, axis=-1, keepdims=True)

    m_all = jnp.maximum(m, m_new)
    w_past = jnp.exp2(m - m_all)
    w_new = jnp.exp2(m_new - m_all)
    o_ref[0] = ((o * w_past + o_new * w_new) / (l * w_past + l_new * w_new)).astype(o_ref.dtype)


def _paged_attention(page_table, qa, qpe, lat_new, kpe_new, cache_lat, cache_kpe_t, wukt, *, layer, tk):
    nb, n_pages = page_table.shape
    assert tk % PAGE_SIZE == 0 and cache_kpe_t.shape[2:] == (QK_ROPE, PAGE_SIZE)
    assert (n_pages * PAGE_SIZE) % (tk * CHUNK_UNROLL) == 0
    t_new = qa.shape[0] // nb
    kv_lora = cache_lat.shape[-1]
    past = n_pages * PAGE_SIZE
    rows = t_new * N_HEADS
    qa3 = qa.reshape(nb, rows, kv_lora)
    qpe3 = qpe.reshape(nb, rows, QK_ROPE)
    ln3 = lat_new.reshape(nb, t_new, kv_lora)
    kn3 = kpe_new.reshape(nb, t_new, QK_ROPE)
    per_seq = lambda r, w: pl.BlockSpec((1, r, w), lambda b, pt: (b, 0, 0))
    const = lambda a: pl.BlockSpec(a.shape, lambda b, pt: (0,) * a.ndim)
    grid_spec = pltpu.PrefetchScalarGridSpec(
        num_scalar_prefetch=1, grid=(nb,),
        in_specs=[per_seq(rows, kv_lora), per_seq(rows, QK_ROPE), per_seq(t_new, kv_lora),
                  per_seq(t_new, QK_ROPE), const(wukt), pl.BlockSpec(memory_space=pl.ANY),
                  pl.BlockSpec(memory_space=pl.ANY)],
        out_specs=per_seq(rows, kv_lora),
        scratch_shapes=[pltpu.VMEM((2, past, kv_lora), F32), pltpu.VMEM((2, n_pages, QK_ROPE, PAGE_SIZE), F32),
                        pltpu.SemaphoreType.DMA((2, 2)),
                        pltpu.VMEM((past, kv_lora), BF16), pltpu.VMEM((CHUNK_UNROLL, kv_lora, tk), BF16),
                        pltpu.VMEM((LANES, kv_lora), BF16),
                        pltpu.VMEM((LANES, QK_ROPE), F32), pltpu.VMEM((past // tk, rows, tk), F32),
                        pltpu.VMEM((N_HEADS * QK_NOPE + rows, kv_lora), BF16)])
    kern = functools.partial(_paged_kernel, layer=layer, n_pages=n_pages, tk=tk)
    out = pl.pallas_call(
        kern, grid_spec=grid_spec,
        out_shape=jax.ShapeDtypeStruct((nb, rows, kv_lora), BF16),
        compiler_params=pltpu.CompilerParams(dimension_semantics=("arbitrary",),
                                             vmem_limit_bytes=VMEM_LIMIT_BYTES),
        name="paged_decode",
    )(page_table.reshape(-1), qa3, qpe3, ln3, kn3, wukt, cache_lat, cache_kpe_t)
    return out.reshape(nb * t_new, N_HEADS * kv_lora)


FF_CHUNK = 256


def _out_ffn_kernel(x_ref, attn_ref, y_ref, gates_ref, wco_ref, womla_ref, wout_ref, gffn_ref,
                    wg_ref, wu_ref, wd_ref, *rest, latent_attn):
    d_model = x_ref.shape[1]
    d_ff = wg_ref.shape[1]
    conv_out = _dot(y_ref[...], wco_ref[...])
    if latent_attn:
        wuv_ref, o_ref, act_ref = rest
        attn = _dot(attn_ref[...], wuv_ref[...]).astype(BF16)
    else:
        o_ref, act_ref = rest
        attn = attn_ref[...]
    mla_out = _dot(attn, womla_ref[...])
    merged = gates_ref[:, :d_model] * conv_out + gates_ref[:, d_model:] * mla_out
    x1 = x_ref[...] + _dot(merged.astype(BF16), wout_ref[...])
    h2 = (x1 * _rms_scale(x1) * gffn_ref[...]).astype(BF16)
    for c0 in range(0, d_ff, FF_CHUNK):
        sl = slice(c0, c0 + FF_CHUNK)
        g = _dot(h2, wg_ref[:, sl])
        act_ref[:, sl] = (g * jax.nn.sigmoid(g) * _dot(h2, wu_ref[:, sl])).astype(BF16)
    o_ref[...] = x1 + _dot(act_ref[...], wd_ref[...])


def _out_ffn(x, attn, y_act, gates, wts, *, tm, latent_attn):
    t, d_model = x.shape
    d_ff = wts["w_gate"].shape[1]
    assert d_ff % FF_CHUNK == 0
    row = lambda w: pl.BlockSpec((tm, w), lambda i: (i, 0))
    wspec = lambda a: _const_spec(a.shape, single_buffer=True)
    weights = [wts["w_conv_out"], wts["w_o_mla"], wts["w_out"], wts["g_ffn"], wts["w_gate"], wts["w_up"],
               wts["w_down"]] + ([wts["w_uv_bd"]] if latent_attn else [])
    return pl.pallas_call(
        functools.partial(_out_ffn_kernel, latent_attn=latent_attn), grid=(t // tm,),
        in_specs=[row(d_model), row(attn.shape[1]), row(y_act.shape[1]), row(2 * d_model)]
                 + [wspec(w) for w in weights],
        out_specs=row(d_model),
        out_shape=jax.ShapeDtypeStruct((t, d_model), F32),
        scratch_shapes=[pltpu.VMEM((tm, d_ff), BF16)],
        compiler_params=pltpu.CompilerParams(dimension_semantics=("arbitrary",),
                                             vmem_limit_bytes=VMEM_LIMIT_BYTES),
        name="out_ffn_decode" if latent_attn else "out_ffn_prompt",
    )(x, attn, y_act, gates, *weights)


def _head_tile_cols(w_rope, w_nope):
    lead = w_nope.shape[:-2]
    pad = jnp.zeros(lead + (N_HEADS, HEAD_TILE - QK_HEAD), w_nope.dtype)
    tile = jnp.concatenate([w_rope, w_nope, pad], axis=-1)
    return tile.reshape(lead + (N_HEADS * HEAD_TILE,))


def _gain_tile(g_compact):
    g = g_compact.astype(F32)
    return jnp.concatenate([g[QK_NOPE:], g[QK_NOPE:], g[:QK_NOPE], jnp.zeros((HEAD_TILE - QK_HEAD,), F32)])[None]


def _rope_tables(pos):
    inv_freq = 1.0 / (ROPE_THETA ** (jnp.arange(0, QK_ROPE, 2, dtype=F32) / QK_ROPE))
    ang = pos.astype(F32)[:, None] * inv_freq[None, :]
    c, s = jnp.cos(ang), jnp.sin(ang)
    n = pos.shape[0]
    z16 = jnp.zeros((n, ROPE_HALF), F32)
    rest = LANES - QK_ROPE
    c_tab = jnp.concatenate([c, c, jnp.ones((n, QK_NOPE), F32), jnp.zeros((n, rest - QK_NOPE), F32)], axis=1)
    sl_tab = jnp.concatenate([-s, z16, jnp.zeros((n, rest), F32)], axis=1)
    sr_tab = jnp.concatenate([z16, s, jnp.zeros((n, rest), F32)], axis=1)
    return c_tab, sl_tab, sr_tab


def _layer_weights(l, norm_mix_g, w_in, q_a_norm_g, w_uq, kv_a_norm_g, w_uk, w_uv, q_norm_g, k_norm_g,
                   w_o_mla, w_conv_out, w_out, norm_ffn_g, w_gate, w_up, w_down):
    d_model = w_in.shape[1]
    kv_lora = w_uk.shape[1]
    q_lora = w_uq.shape[1]
    d_conv = w_conv_out.shape[1]
    o1 = 2 * d_conv
    o2 = o1 + q_lora
    o3 = o2 + kv_lora
    o4 = o3 + QK_ROPE
    wi = w_in[l]
    w_in_p = jnp.concatenate([wi[:, :o3], wi[:, o3:o4], jnp.zeros((d_model, HEAD_TILE - QK_ROPE), wi.dtype),
                              wi[:, o4:]], axis=1).astype(BF16)
    uq = w_uq[l].reshape(q_lora, N_HEADS, QK_HEAD)
    w_uq_p = _head_tile_cols(uq[..., QK_NOPE:], uq[..., :QK_NOPE]).astype(BF16)
    uk = w_uk[l]
    w_uk_p = _head_tile_cols(jnp.zeros((kv_lora, N_HEADS, QK_ROPE), uk.dtype), uk).astype(BF16)
    w_ukt = jnp.transpose(uk, (2, 1, 0)).reshape(QK_NOPE * N_HEADS, kv_lora).astype(BF16)
    uv = w_uv[l]
    w_uv_p = jnp.concatenate([uv, jnp.zeros((kv_lora, N_HEADS, HEAD_TILE - V_HEAD), uv.dtype)],
                             axis=-1).reshape(kv_lora, N_HEADS * HEAD_TILE).astype(BF16)
    lane = jnp.arange(HEAD_TILE)
    bound = (1.02 * LOG2E * math.sqrt(QK_HEAD)) * jnp.max(jnp.abs(q_norm_g[l])) * jnp.max(jnp.abs(k_norm_g[l]))
    q_one = jnp.where(lane == BIAS_LANE, 1.0, 0.0).astype(F32)[None]
    k_bias = jnp.where(lane == BIAS_LANE, -bound, 0.0).astype(F32)[None]
    v_one = jnp.tile(jnp.where(lane == ONES_LANE, 1.0, 0.0).astype(F32), N_HEADS)[None]
    ukt = jnp.transpose(uk, (1, 2, 0))
    w_abs = jnp.concatenate([jnp.zeros((N_HEADS, QK_ROPE, kv_lora), uk.dtype), ukt,
                             jnp.zeros((N_HEADS, HEAD_TILE - QK_HEAD, kv_lora), uk.dtype)], axis=1).astype(BF16)
    lane_ids = jnp.arange(HEAD_TILE)[None, :, None]
    col_ids = jnp.arange(N_HEADS * QK_ROPE)[None, None, :]
    head_ids = jnp.arange(N_HEADS)[:, None, None]
    w_sel = ((lane_ids < QK_ROPE) & (col_ids == head_ids * QK_ROPE + lane_ids)).astype(BF16)
    eye_h = jnp.eye(N_HEADS, dtype=uv.dtype)
    w_uv_bd = jnp.einsum("chd,hg->hcgd", uv, eye_h).reshape(N_HEADS * kv_lora, N_HEADS * V_HEAD).astype(BF16)
    g_k = _gain_tile(k_norm_g[l])
    return dict(
        g_mix=norm_mix_g[l][None].astype(F32), w_in=w_in_p, g_qa=q_a_norm_g[l][None].astype(F32), w_uq=w_uq_p,
        g_kva=kv_a_norm_g[l][None].astype(F32), g_q=_gain_tile(q_norm_g[l]), g_k=g_k,
        w_uk=w_uk_p, w_uv_p=w_uv_p, w_uv_bd=w_uv_bd, q_one=q_one, k_bias=k_bias, v_one=v_one, score_bound=bound,
        w_ukt=w_ukt, w_abs=w_abs, w_sel=w_sel, w_conv_out=w_conv_out[l].astype(BF16),
        w_o_mla=w_o_mla[l].astype(BF16),
        w_out=w_out[l].astype(BF16), g_ffn=norm_ffn_g[l][None].astype(F32), w_gate=w_gate[l].astype(BF16),
        w_up=w_up[l].astype(BF16), w_down=w_down[l].astype(BF16))


def kernel(x_prompt, x_sample, cache_kv_latent, cache_k_rope, state_conv, page_table, norm_mix_g, w_in,
           q_a_norm_g, w_uq, kv_a_norm_g, w_uk, w_uv, q_norm_g, k_norm_g, w_o_mla, conv_w, conv_b, conv_ln_g,
           conv_ln_b, w_conv_out, w_out, norm_ffn_g, w_gate, w_up, w_down):
    batch, seq, d_model = x_prompt.shape
    nb, t_new, _ = x_sample.shape
    depth = w_in.shape[0]
    n_pages = page_table.shape[1]
    past = n_pages * PAGE_SIZE
    n_state = state_conv.shape[2]
    d_conv = state_conv.shape[3]
    kv_lora = cache_kv_latent.shape[-1]

    tabs_p = _rope_tables(jnp.arange(seq))
    tabs_s = _rope_tables(jnp.tile(past + jnp.arange(t_new), nb))

    yp = x_prompt.reshape(batch * seq, d_model)
    ys = x_sample.reshape(nb * t_new, d_model)
    outs = [[] for _ in range(6)]
    for l in range(depth):
        wts = _layer_weights(l, norm_mix_g, w_in, q_a_norm_g, w_uq, kv_a_norm_g, w_uk, w_uv, q_norm_g,
                             k_norm_g, w_o_mla, w_conv_out, w_out, norm_ffn_g, w_gate, w_up, w_down)
        cb, lg, lb = (a[l][None].astype(F32) for a in (conv_b, conv_ln_g, conv_ln_b))

        u, lat, kpe, gates, q, k, v = _in_proj(yp, tabs_p, wts, tm=512, decode=False)
        y_act = _conv_prompt(u, conv_w[l].astype(F32), cb, lg, lb, batch=batch, ts=512)
        attn = lax.cond(wts["score_bound"] <= FIXED_SHIFT_MAX_BOUND,
                        functools.partial(_flash_attention, batch=batch, tq=512, heads_per_step=N_HEADS,
                                          online=False),
                        functools.partial(_flash_attention, batch=batch, tq=512, heads_per_step=2,
                                          online=True), q, k, v)
        yp = _out_ffn(yp, attn, y_act, gates, wts, tm=512, latent_attn=False)
        outs[0].append(lat.reshape(batch, seq, kv_lora))
        outs[1].append(kpe.reshape(batch, seq, QK_ROPE))
        outs[2].append(u.reshape(batch, seq, d_conv)[:, seq - n_state:])

        u, lat, kpe, gates, qa, qpe = _in_proj(ys, tabs_s, wts, tm=nb * t_new, decode=True)
        u_t = jnp.transpose(u.reshape(nb, t_new, d_conv), (1, 0, 2))
        state_t = jnp.transpose(state_conv[l], (1, 0, 2))
        y_act = _conv_decode(state_t, u_t, conv_w[l].astype(F32), cb, lg, lb)
        y_act = jnp.transpose(y_act, (1, 0, 2)).reshape(nb * t_new, d_conv)
        attn = _paged_attention(page_table, qa, qpe, lat, kpe, cache_kv_latent, jnp.swapaxes(cache_k_rope, 2, 3),
                                wts["w_ukt"], layer=l, tk=2048)
        ys = _out_ffn(ys, attn, y_act, gates, wts, tm=nb * t_new, latent_attn=True)
        outs[3].append(lat.reshape(nb, t_new, kv_lora))
        outs[4].append(kpe.reshape(nb, t_new, QK_ROPE))
        outs[5].append(jnp.transpose(jnp.concatenate([state_t[t_new:], u_t], axis=0), (1, 0, 2)))

    return (yp.reshape(batch, seq, d_model), ys.reshape(nb, t_new, d_model),
            *(jnp.stack(o) for o in outs))
```

```python
import functools
import math

import jax
import jax.numpy as jnp
from jax import lax
from jax.experimental import pallas as pl
from jax.experimental.pallas import tpu as pltpu

N_HEADS = 8
QK_NOPE = 64
QK_ROPE = 32
ROPE_HALF = QK_ROPE // 2
QK_HEAD = QK_NOPE + QK_ROPE
V_HEAD = 64
ROPE_THETA = 10000.0
EPS = 1e-6
PAGE_SIZE = 128

LANES = 128
HEAD_TILE = LANES
NOPE_OFF = QK_ROPE
VMEM_LIMIT_BYTES = 56 * 1024 * 1024

BIAS_LANE = QK_HEAD
ONES_LANE = V_HEAD
LOG2E = 1.4426950408889634
FIXED_SHIFT_MAX_BOUND = 40.0

F32 = jnp.float32
BF16 = jnp.bfloat16
NEG_BIG = -1e30


def _dot(a, b):
    return jnp.dot(a, b, preferred_element_type=F32)


def _dot_nt(a, b):
    return lax.dot_general(a, b, (((1,), (1,)), ((), ())), preferred_element_type=F32)


def _rms_scale(x):
    return lax.rsqrt(jnp.mean(x * x, axis=-1, keepdims=True) + EPS)


def _const_spec(shape, single_buffer=False):
    nd = len(shape)
    if single_buffer:
        return pl.BlockSpec(shape, lambda *_: (0,) * nd, pipeline_mode=pl.Buffered(1))
    return pl.BlockSpec(shape, lambda *_: (0,) * nd)


def _rope_tile(t, c_tab, sl_tab, sr_tab):
    return t * c_tab + pltpu.roll(t, ROPE_HALF, 1) * sr_tab + pltpu.roll(t, LANES - ROPE_HALF, 1) * sl_tab


def _in_proj_kernel(x_ref, ctab_ref, sltab_ref, srtab_ref, gmix_ref, w_in_ref, gqa_ref, w_uq_ref,
                    gkva_ref, gq_ref, gk_ref, qone_ref, kbias_ref, vone_ref, w_uk_ref, w_uv_ref,
                    wabs_ref, sel_ref, u_ref, lat_ref, kpe_ref, gates_ref, *attn_refs,
                    d_conv, q_lora, kv_lora, d_model, decode):
    x = x_ref[...]
    hb = (x * _rms_scale(x) * gmix_ref[...]).astype(BF16)
    c_tab, sl_tab, sr_tab = ctab_ref[...], sltab_ref[...], srtab_ref[...]

    o_cq = 2 * d_conv
    o_ckv = o_cq + q_lora
    o_kpe = o_ckv + kv_lora
    o_gate = o_kpe + HEAD_TILE

    ag = _dot(hb, w_in_ref[:, 0:o_cq])
    u_ref[...] = ag[:, :d_conv] * jax.nn.sigmoid(ag[:, d_conv:])

    cq = _dot(hb, w_in_ref[:, o_cq:o_ckv])
    cqn = (cq * _rms_scale(cq) * gqa_ref[...]).astype(BF16)
    q = _dot(cqn, w_uq_ref[...])
    q_scale = LOG2E / math.sqrt(QK_HEAD)
    qpe = None
    for h in range(N_HEADS):
        sl = slice(h * HEAD_TILE, (h + 1) * HEAD_TILE)
        qh = _rope_tile(q[:, sl], c_tab, sl_tab, sr_tab)
        ss = jnp.sum(qh * qh, axis=-1, keepdims=True)
        qh = qh * lax.rsqrt(ss * (1.0 / QK_HEAD) + EPS) * (gq_ref[...] * q_scale)
        if decode:
            qa_ref, qpe_ref = attn_refs
            qg = (qh * gk_ref[...]).astype(BF16)
            qa_ref[:, h * kv_lora:(h + 1) * kv_lora] = _dot(qg, wabs_ref[h]).astype(qa_ref.dtype)
            part = _dot(qg, sel_ref[h])
            qpe = part if qpe is None else qpe + part
        else:
            attn_refs[0][:, sl] = (qh + qone_ref[...]).astype(attn_refs[0].dtype)
    if decode:
        qpe_ref[...] = qpe.astype(qpe_ref.dtype)

    ck = _dot(hb, w_in_ref[:, o_ckv:o_gate])
    ckv = ck[:, :kv_lora]
    lat = ckv * _rms_scale(ckv) * gkva_ref[...]
    lat_ref[...] = lat
    kpe = _rope_tile(ck[:, kv_lora:], c_tab, sl_tab, sr_tab)
    kpe_ref[...] = kpe[:, :QK_ROPE]

    for g in range(2):
        sl = slice(g * d_model, (g + 1) * d_model)
        gate = jax.nn.sigmoid(_dot(hb, w_in_ref[:, o_gate + g * d_model:o_gate + (g + 1) * d_model]))
        gates_ref[:, sl] = gate.astype(gates_ref.dtype)

    if not decode:
        _, k_ref, v_ref = attn_refs
        latb = lat.astype(BF16)
        v_ref[...] = (_dot(latb, w_uv_ref[...]) + vone_ref[...]).astype(v_ref.dtype)
        kn = _dot(latb, w_uk_ref[...])
        for h in range(N_HEADS):
            sl = slice(h * HEAD_TILE, (h + 1) * HEAD_TILE)
            kh = kn[:, sl] + kpe
            ss = jnp.sum(kh * kh, axis=-1, keepdims=True)
            kh = kh * lax.rsqrt(ss * (1.0 / QK_HEAD) + EPS) * gk_ref[...] + kbias_ref[...]
            k_ref[:, sl] = kh.astype(k_ref.dtype)


def _in_proj(x, tabs, wts, *, tm, decode):
    t, d_model = x.shape
    n_tab_tiles = tabs[0].shape[0] // tm
    d_in_p = wts["w_in"].shape[1]
    q_lora = wts["w_uq"].shape[0]
    kv_lora = wts["w_uk"].shape[0]
    d_conv = (d_in_p - q_lora - kv_lora - HEAD_TILE - 2 * d_model) // 2
    qw = N_HEADS * HEAD_TILE
    vw = N_HEADS * HEAD_TILE

    row = lambda w: pl.BlockSpec((tm, w), lambda i: (i, 0))
    tab = pl.BlockSpec((tm, LANES), lambda i: (i % n_tab_tiles, 0))
    in_specs = [row(d_model), tab, tab, tab,
                _const_spec((1, d_model)), _const_spec(wts["w_in"].shape),
                _const_spec((1, q_lora)), _const_spec(wts["w_uq"].shape),
                _const_spec((1, kv_lora)), _const_spec((1, HEAD_TILE)), _const_spec((1, HEAD_TILE)),
                _const_spec((1, HEAD_TILE)), _const_spec((1, HEAD_TILE)), _const_spec((1, vw)),
                _const_spec(wts["w_uk"].shape), _const_spec(wts["w_uv_p"].shape),
                _const_spec(wts["w_abs"].shape), _const_spec(wts["w_sel"].shape)]
    out_shape = [jax.ShapeDtypeStruct((t, d_conv), F32), jax.ShapeDtypeStruct((t, kv_lora), F32),
                 jax.ShapeDtypeStruct((t, QK_ROPE), F32), jax.ShapeDtypeStruct((t, 2 * d_model), BF16)]
    out_specs = [row(d_conv), row(kv_lora), row(QK_ROPE), row(2 * d_model)]
    if decode:
        widths = [N_HEADS * kv_lora, N_HEADS * QK_ROPE]
    else:
        widths = [qw, qw, vw]
    out_shape += [jax.ShapeDtypeStruct((t, w), BF16) for w in widths]
    out_specs += [row(w) for w in widths]
    kern = functools.partial(_in_proj_kernel, d_conv=d_conv, q_lora=q_lora, kv_lora=kv_lora,
                             d_model=d_model, decode=decode)
    return pl.pallas_call(
        kern, grid=(t // tm,), in_specs=in_specs, out_specs=out_specs, out_shape=out_shape,
        compiler_params=pltpu.CompilerParams(dimension_semantics=("arbitrary",),
                                             vmem_limit_bytes=VMEM_LIMIT_BYTES),
        name="in_proj_decode" if decode else "in_proj_prompt",
    )(x, *tabs, wts["g_mix"], wts["w_in"], wts["g_qa"], wts["w_uq"], wts["g_kva"], wts["g_q"], wts["g_k"],
      wts["q_one"], wts["k_bias"], wts["v_one"], wts["w_uk"], wts["w_uv_p"], wts["w_abs"], wts["w_sel"])


def _ln_silu(y, g, b):
    mu = jnp.mean(y, axis=-1, keepdims=True)
    yc = y - mu
    yn = yc * lax.rsqrt(jnp.mean(yc * yc, axis=-1, keepdims=True) + EPS) * g + b
    return yn * jax.nn.sigmoid(yn)


HALO = 32


SUBLANES = 8


def _conv_prompt_kernel(cur_ref, prev_ref, w_ref, b_ref, lng_ref, lnb_ref, y_ref, ext_ref, sh_ref, *, rows):
    ts = cur_ref.shape[1]
    width = w_ref.shape[0]
    ext_ref[HALO:, :] = cur_ref[0]
    prev = prev_ref[0]
    ext_ref[0:HALO, :] = jnp.where(pl.program_id(1) > 0, prev, jnp.zeros_like(prev))
    span = sh_ref.shape[1]
    for r in range(1, SUBLANES):
        sh_ref[r - 1] = ext_ref[r:r + span, :]
    base = HALO - (width - 1)
    for r0 in range(0, ts, rows):
        acc = None
        for k in range(width):
            kk = base + k
            start = r0 + (kk // SUBLANES) * SUBLANES
            if kk % SUBLANES == 0:
                window = ext_ref[start:start + rows, :]
            else:
                window = sh_ref[kk % SUBLANES - 1, start:start + rows, :]
            term = window * w_ref[k:k + 1, :]
            acc = term if acc is None else acc + term
        y = _ln_silu(acc + b_ref[...], lng_ref[...], lnb_ref[...])
        y_ref[r0:r0 + rows, :] = y.astype(y_ref.dtype)


def _conv_prompt(u, conv_w, conv_b, ln_g, ln_b, *, batch, ts, rows=64):
    t, d_conv = u.shape
    seq = t // batch
    width = conv_w.shape[0]
    assert width - 1 <= HALO and ts % HALO == 0 and seq % ts == 0 and HALO % SUBLANES == 0
    u3 = u.reshape(batch, seq, d_conv)
    nst = seq // ts
    span = ts + HALO - SUBLANES
    kern = functools.partial(_conv_prompt_kernel, rows=rows)
    return pl.pallas_call(
        kern, grid=(batch, nst),
        in_specs=[pl.BlockSpec((1, ts, d_conv), lambda b, i: (b, i, 0)),
                  pl.BlockSpec((1, HALO, d_conv), lambda b, i: (b, jnp.maximum(i * (ts // HALO) - 1, 0), 0)),
                  _const_spec(conv_w.shape), _const_spec((1, d_conv)), _const_spec((1, d_conv)),
                  _const_spec((1, d_conv))],
        out_specs=pl.BlockSpec((ts, d_conv), lambda b, i: (b * nst + i, 0)),
        out_shape=jax.ShapeDtypeStruct((t, d_conv), BF16),
        scratch_shapes=[pltpu.VMEM((ts + HALO, d_conv), F32), pltpu.VMEM((SUBLANES - 1, span, d_conv), F32)],
        compiler_params=pltpu.CompilerParams(dimension_semantics=("arbitrary", "arbitrary"),
                                             vmem_limit_bytes=VMEM_LIMIT_BYTES),
        name="conv_prompt",
    )(u3, u3, conv_w, conv_b, ln_g, ln_b)


def _conv_decode_kernel(state_ref, unew_ref, w_ref, b_ref, lng_ref, lnb_ref, y_ref):
    n_state = state_ref.shape[0]
    for t in range(unew_ref.shape[0]):
        acc = None
        for k in range(w_ref.shape[0]):
            j = t + k
            src = state_ref[j] if j < n_state else unew_ref[j - n_state]
            term = src * w_ref[k:k + 1, :]
            acc = term if acc is None else acc + term
        y_ref[t] = _ln_silu(acc + b_ref[...], lng_ref[...], lnb_ref[...]).astype(y_ref.dtype)


def _conv_decode(state_t, u_new_t, conv_w, conv_b, ln_g, ln_b):
    t_new, nb, d_conv = u_new_t.shape
    assert state_t.shape[0] == conv_w.shape[0] - 1
    full = lambda a: _const_spec(a.shape)
    return pl.pallas_call(
        _conv_decode_kernel, grid=(1,),
        in_specs=[full(state_t), full(u_new_t), full(conv_w), _const_spec((1, d_conv)),
                  _const_spec((1, d_conv)), _const_spec((1, d_conv))],
        out_specs=_const_spec((t_new, nb, d_conv)),
        out_shape=jax.ShapeDtypeStruct((t_new, nb, d_conv), BF16),
        compiler_params=pltpu.CompilerParams(dimension_semantics=("arbitrary",),
                                             vmem_limit_bytes=VMEM_LIMIT_BYTES),
        name="conv_decode",
    )(state_t, u_new_t, conv_w, conv_b, ln_g, ln_b)


def _flash_kernel(qi_tab, ki_tab, q_ref, k_ref, v_ref, o_ref, acc_ref, *m_refs, online):
    step = pl.program_id(2)
    qi = qi_tab[step]
    ki = ki_tab[step]
    tq = q_ref.shape[1]
    tk = k_ref.shape[1]
    n_heads = acc_ref.shape[0]

    @pl.when(ki == 0)
    def _():
        acc_ref[...] = jnp.zeros_like(acc_ref)
        if online:
            m_refs[0][...] = jnp.full_like(m_refs[0], NEG_BIG)

    def accumulate(masked):
        if masked:
            visible = (lax.broadcasted_iota(jnp.int32, (tq, tk), 1)
                       <= lax.broadcasted_iota(jnp.int32, (tq, tk), 0))
        for hh in range(n_heads):
            sl = slice(hh * HEAD_TILE, (hh + 1) * HEAD_TILE)
            s = _dot_nt(q_ref[0, :, sl], k_ref[0, :, sl])
            if masked:
                s = jnp.where(visible, s, NEG_BIG)
            if online:
                m_old = m_refs[0][hh]
                m_new = jnp.maximum(m_old, jnp.max(s, axis=-1, keepdims=True))
                m_refs[0][hh] = m_new
                p = jnp.exp2(s - m_new).astype(BF16)
                acc_ref[hh] = jnp.exp2(m_old - m_new) * acc_ref[hh] + _dot(p, v_ref[0, :, sl])
            else:
                acc_ref[hh] += _dot(jnp.exp2(s).astype(BF16), v_ref[0, :, sl])

    @pl.when(ki < qi)
    def _():
        accumulate(False)

    @pl.when(ki == qi)
    def _():
        accumulate(True)
        lane = lax.broadcasted_iota(jnp.int32, (tq, HEAD_TILE), 1)
        for pair in range(n_heads // 2):
            a0, a1 = acc_ref[2 * pair], acc_ref[2 * pair + 1]
            h0 = a0 / a0[:, ONES_LANE:ONES_LANE + 1]
            h1 = a1 / a1[:, ONES_LANE:ONES_LANE + 1]
            o_ref[0, :, pair * HEAD_TILE:(pair + 1) * HEAD_TILE] = jnp.where(
                lane < V_HEAD, h0, pltpu.roll(h1, V_HEAD, 1)).astype(o_ref.dtype)


def _flash_attention(q, k, v, *, batch, tq, heads_per_step, online):
    t = q.shape[0]
    seq = t // batch
    nq = seq // tq
    assert 2 * V_HEAD == HEAD_TILE and heads_per_step % 2 == 0 and N_HEADS % heads_per_step == 0
    q3, k3, v3 = (a.reshape(batch, seq, a.shape[1]) for a in (q, k, v))
    qi_list = [qi for qi in range(nq) for _ in range(qi + 1)]
    ki_list = [ki for qi in range(nq) for ki in range(qi + 1)]
    qi_tab = jnp.asarray(qi_list, jnp.int32)
    ki_tab = jnp.asarray(ki_list, jnp.int32)
    pair_tile = lambda tab: pl.BlockSpec((1, tq, heads_per_step * HEAD_TILE),
                                         lambda b, hp, s, qt, kt: (b, (qt if tab == "q" else kt)[s], hp))
    scratch = [pltpu.VMEM((heads_per_step, tq, HEAD_TILE), F32)]
    if online:
        scratch.append(pltpu.VMEM((heads_per_step, tq, 1), F32))
    grid_spec = pltpu.PrefetchScalarGridSpec(
        num_scalar_prefetch=2, grid=(batch, N_HEADS // heads_per_step, len(qi_list)),
        in_specs=[pair_tile("q"), pair_tile("k"), pair_tile("k")],
        out_specs=pl.BlockSpec((1, tq, heads_per_step * V_HEAD), lambda b, hp, s, qt, kt: (b, qt[s], hp)),
        scratch_shapes=scratch)
    out = pl.pallas_call(
        functools.partial(_flash_kernel, online=online), grid_spec=grid_spec,
        out_shape=jax.ShapeDtypeStruct((batch, seq, N_HEADS * V_HEAD), BF16),
        compiler_params=pltpu.CompilerParams(dimension_semantics=("arbitrary",) * 3,
                                             vmem_limit_bytes=VMEM_LIMIT_BYTES),
        name="flash_prompt_online" if online else "flash_prompt",
    )(qi_tab, ki_tab, q3, k3, v3)
    return out.reshape(t, N_HEADS * V_HEAD)


CHUNK_UNROLL = 2


def _paged_kernel(pt_ref, qa_ref, qpe_ref, latn_ref, kpen_ref, wukt_ref, lat_hbm, kpet_hbm,
                  o_ref, lat_buf, kpet_buf, sems, cb_ref, cbt_ref, cbn_ref, kpn_ref, s_ref, lhs_ref, *,
                  layer, n_pages, tk):
    b = pl.program_id(0)
    nb = pl.num_programs(0)
    past = n_pages * PAGE_SIZE
    n_chunks = past // tk
    t_new = latn_ref.shape[1]
    rows = qa_ref.shape[1]

    def page_copies(seq, slot, j):
        page = pt_ref[seq * n_pages + j]
        dst = pl.ds(j * PAGE_SIZE, PAGE_SIZE)
        return (pltpu.make_async_copy(lat_hbm.at[layer, page], lat_buf.at[slot, dst], sems.at[slot, 0]),
                pltpu.make_async_copy(kpet_hbm.at[layer, page], kpet_buf.at[slot, j], sems.at[slot, 1]))

    def start_fetch(seq, slot):
        for j in range(n_pages):
            for cp in page_copies(seq, slot, j):
                cp.start()

    def wait_fetch(seq, slot):
        for j in range(n_pages):
            for cp in page_copies(seq, slot, j):
                cp.wait()

    slot = b % 2

    @pl.when(b == 0)
    def _():
        start_fetch(0, 0)
        lhs_ref[0:N_HEADS * QK_NOPE, :] = wukt_ref[...]
        cbn_ref[...] = jnp.zeros_like(cbn_ref)
        kpn_ref[...] = jnp.zeros_like(kpn_ref)

    @pl.when(b + 1 < nb)
    def _():
        start_fetch(b + 1, 1 - slot)

    lhs_ref[N_HEADS * QK_NOPE:, :] = qa_ref[0]
    qpe = qpe_ref[0]
    ones = jnp.ones((N_HEADS, QK_ROPE), BF16)
    pages_per_chunk = tk // PAGE_SIZE

    def scores(cbt, ss_pe, s_pe):
        r = _dot(lhs_ref[...], cbt)
        n = cbt.shape[1]
        kn = r[:N_HEADS * QK_NOPE]
        ss = jnp.sum((kn * kn).reshape(QK_NOPE, N_HEADS, n), axis=0) + ss_pe
        inv = lax.rsqrt(ss * (1.0 / QK_HEAD) + EPS)
        return (r[N_HEADS * QK_NOPE:] + s_pe) * jnp.concatenate([inv] * t_new, axis=0)

    wait_fetch(b, slot)

    def lane_tiles(a):
        return [a[:, i:i + LANES] for i in range(0, a.shape[1], LANES)]

    def chunk_group(g, m):
        for j in range(CHUNK_UNROLL):
            c = g * CHUNK_UNROLL + j
            st = pl.multiple_of(c * tk, tk)
            cb = lat_buf[slot, pl.ds(st, tk), :].astype(BF16)
            cb_ref[pl.ds(st, tk), :] = cb
            kpt = jnp.concatenate([kpet_buf[slot, c * pages_per_chunk + i] for i in range(pages_per_chunk)],
                                  axis=1)
            cbt_ref[j] = cb.T
            sc = scores(cbt_ref[j], jnp.sum(kpt * kpt, axis=0, keepdims=True), _dot(qpe, kpt.astype(BF16)))
            s_ref[c] = sc
            m = functools.reduce(jnp.maximum, [m] + lane_tiles(sc))
        return m

    m = lax.fori_loop(0, n_chunks // CHUNK_UNROLL, chunk_group, jnp.full((rows, LANES), NEG_BIG, F32))
    m = jnp.max(m, axis=-1, keepdims=True)

    cbn_ref[0:t_new, :] = latn_ref[0].astype(BF16)
    kpn_ref[0:t_new, :] = kpen_ref[0]
    cbn = cbn_ref[...]
    kpn = kpn_ref[...]
    sq = kpn * kpn
    hi = sq.astype(BF16)
    lo = (sq - hi.astype(F32)).astype(BF16)
    col = lax.broadcasted_iota(jnp.int32, (rows, LANES), 1)
    qrow = lax.broadcasted_iota(jnp.int32, (rows, LANES), 0) // N_HEADS
    s_new = scores(cbn.astype(F32).T.astype(BF16), _dot_nt(ones, hi) + _dot_nt(ones, lo),
                   _dot_nt(qpe, kpn.astype(BF16)))
    s_new = jnp.where(col <= qrow, s_new, NEG_BIG)
    m_new = jnp.max(s_new, axis=-1, keepdims=True)
    p_new = jnp.exp2(s_new - m_new)
    l_new = jnp.sum(p_new, axis=-1, keepdims=True)
    o_new = _dot(p_new.astype(BF16), cbn)

    l = jnp.zeros((rows, LANES), F32)
    o = jnp.zeros((rows, cb_ref.shape[1]), F32)
    for c in range(n_chunks):
        p = jnp.exp2(s_ref[c] - m)
        l = functools.reduce(jnp.add, [l] + lane_tiles(p))
        o = o + _dot(p.astype(BF16), cb_ref[c * tk:(c + 1) * tk, :])
    l = jnp.sum(l, axis=-1, keepdims=True)

    m_all = jnp.maximum(m, m_new)
    w_past = jnp.exp2(m - m_all)
    w_new = jnp.exp2(m_new - m_all)
    o_ref[0] = ((o * w_past + o_new * w_new) / (l * w_past + l_new * w_new)).astype(o_ref.dtype)


def _paged_attention(page_table, qa, qpe, lat_new, kpe_new, cache_lat, cache_kpe_t, wukt, *, layer, tk):
    nb, n_pages = page_table.shape
    assert tk % PAGE_SIZE == 0 and cache_kpe_t.shape[2:] == (QK_ROPE, PAGE_SIZE)
    assert (n_pages * PAGE_SIZE) % (tk * CHUNK_UNROLL) == 0
    t_new = qa.shape[0] // nb
    kv_lora = cache_lat.shape[-1]
    past = n_pages * PAGE_SIZE
    rows = t_new * N_HEADS
    qa3 = qa.reshape(nb, rows, kv_lora)
    qpe3 = qpe.reshape(nb, rows, QK_ROPE)
    ln3 = lat_new.reshape(nb, t_new, kv_lora)
    kn3 = kpe_new.reshape(nb, t_new, QK_ROPE)
    per_seq = lambda r, w: pl.BlockSpec((1, r, w), lambda b, pt: (b, 0, 0))
    const = lambda a: pl.BlockSpec(a.shape, lambda b, pt: (0,) * a.ndim)
    grid_spec = pltpu.PrefetchScalarGridSpec(
        num_scalar_prefetch=1, grid=(nb,),
        in_specs=[per_seq(rows, kv_lora), per_seq(rows, QK_ROPE), per_seq(t_new, kv_lora),
                  per_seq(t_new, QK_ROPE), const(wukt), pl.BlockSpec(memory_space=pl.ANY),
                  pl.BlockSpec(memory_space=pl.ANY)],
        out_specs=per_seq(rows, kv_lora),
        scratch_shapes=[pltpu.VMEM((2, past, kv_lora), F32), pltpu.VMEM((2, n_pages, QK_ROPE, PAGE_SIZE), F32),
                        pltpu.SemaphoreType.DMA((2, 2)),
                        pltpu.VMEM((past, kv_lora), BF16), pltpu.VMEM((CHUNK_UNROLL, kv_lora, tk), BF16),
                        pltpu.VMEM((LANES, kv_lora), BF16),
                        pltpu.VMEM((LANES, QK_ROPE), F32), pltpu.VMEM((past // tk, rows, tk), F32),
                        pltpu.VMEM((N_HEADS * QK_NOPE + rows, kv_lora), BF16)])
    kern = functools.partial(_paged_kernel, layer=layer, n_pages=n_pages, tk=tk)
    out = pl.pallas_call(
        kern, grid_spec=grid_spec,
        out_shape=jax.ShapeDtypeStruct((nb, rows, kv_lora), BF16),
        compiler_params=pltpu.CompilerParams(dimension_semantics=("arbitrary",),
                                             vmem_limit_bytes=VMEM_LIMIT_BYTES),
        name="paged_decode",
    )(page_table.reshape(-1), qa3, qpe3, ln3, kn3, wukt, cache_lat, cache_kpe_t)
    return out.reshape(nb * t_new, N_HEADS * kv_lora)


FF_CHUNK = 256


def _out_ffn_kernel(x_ref, attn_ref, y_ref, gates_ref, wco_ref, womla_ref, wout_ref, gffn_ref,
                    wg_ref, wu_ref, wd_ref, *rest, latent_attn):
    if latent_attn:
        wuv_ref, o_ref, act_ref = rest
        attn = _dot(attn_ref[...], wuv_ref[...]).astype(BF16)
    else:
        o_ref, act_ref = rest
        attn = attn_ref[...]
    y_cur = y_ref[...]
    d_model = x_ref.shape[1]
    d_ff = wg_ref.shape[1]
    half = d_model // 2
    for c0 in (0, half):
        sl = slice(c0, c0 + half)
        merged = (gates_ref[:, sl] * _dot(y_cur, wco_ref[:, sl])
                  + gates_ref[:, d_model + c0:d_model + c0 + half] * _dot(attn, womla_ref[:, sl]))
        act_ref[:, sl] = merged.astype(BF16)
    o_ref[...] = x_ref[...] + _dot(act_ref[:, :d_model], wout_ref[...])
    x1 = o_ref[...]
    h2 = (x1 * _rms_scale(x1) * gffn_ref[...]).astype(BF16)
    for c0 in range(0, d_ff, FF_CHUNK):
        sl = slice(c0, c0 + FF_CHUNK)
        g = _dot(h2, wg_ref[:, sl])
        act_ref[:, sl] = (g * jax.nn.sigmoid(g) * _dot(h2, wu_ref[:, sl])).astype(BF16)
    o_ref[...] += _dot(act_ref[...], wd_ref[...])


def _out_ffn(x, attn, y_act, gates, wts, *, tm, latent_attn):
    t, d_model = x.shape
    d_ff = wts["w_gate"].shape[1]
    assert d_ff % FF_CHUNK == 0 and d_ff >= d_model
    row = lambda w: pl.BlockSpec((tm, w), lambda i: (i, 0))
    wspec = lambda a: _const_spec(a.shape, single_buffer=True)
    weights = [wts["w_conv_out"], wts["w_o_mla"], wts["w_out"], wts["g_ffn"], wts["w_gate"], wts["w_up"],
               wts["w_down"]] + ([wts["w_uv_bd"]] if latent_attn else [])
    return pl.pallas_call(
        functools.partial(_out_ffn_kernel, latent_attn=latent_attn), grid=(t // tm,),
        in_specs=[row(d_model), row(attn.shape[1]), row(y_act.shape[1]), row(2 * d_model)]
                 + [wspec(w) for w in weights],
        out_specs=row(d_model),
        out_shape=jax.ShapeDtypeStruct((t, d_model), F32),
        scratch_shapes=[pltpu.VMEM((tm, d_ff), BF16)],
        compiler_params=pltpu.CompilerParams(dimension_semantics=("arbitrary",),
                                             vmem_limit_bytes=VMEM_LIMIT_BYTES),
        name="out_ffn_decode" if latent_attn else "out_ffn_prompt",
    )(x, attn, y_act, gates, *weights)


def _head_tile_cols(w_rope, w_nope):
    lead = w_nope.shape[:-2]
    pad = jnp.zeros(lead + (N_HEADS, HEAD_TILE - QK_HEAD), w_nope.dtype)
    tile = jnp.concatenate([w_rope, w_nope, pad], axis=-1)
    return tile.reshape(lead + (N_HEADS * HEAD_TILE,))


def _gain_tile(g_compact):
    g = g_compact.astype(F32)
    return jnp.concatenate([g[QK_NOPE:], g[QK_NOPE:], g[:QK_NOPE], jnp.zeros((HEAD_TILE - QK_HEAD,), F32)])[None]


def _rope_tables(pos):
    inv_freq = 1.0 / (ROPE_THETA ** (jnp.arange(0, QK_ROPE, 2, dtype=F32) / QK_ROPE))
    ang = pos.astype(F32)[:, None] * inv_freq[None, :]
    c, s = jnp.cos(ang), jnp.sin(ang)
    n = pos.shape[0]
    z16 = jnp.zeros((n, ROPE_HALF), F32)
    rest = LANES - QK_ROPE
    c_tab = jnp.concatenate([c, c, jnp.ones((n, QK_NOPE), F32), jnp.zeros((n, rest - QK_NOPE), F32)], axis=1)
    sl_tab = jnp.concatenate([-s, z16, jnp.zeros((n, rest), F32)], axis=1)
    sr_tab = jnp.concatenate([z16, s, jnp.zeros((n, rest), F32)], axis=1)
    return c_tab, sl_tab, sr_tab


def _layer_weights(l, norm_mix_g, w_in, q_a_norm_g, w_uq, kv_a_norm_g, w_uk, w_uv, q_norm_g, k_norm_g,
                   w_o_mla, w_conv_out, w_out, norm_ffn_g, w_gate, w_up, w_down):
    d_model = w_in.shape[1]
    kv_lora = w_uk.shape[1]
    q_lora = w_uq.shape[1]
    d_conv = w_conv_out.shape[1]
    o1 = 2 * d_conv
    o2 = o1 + q_lora
    o3 = o2 + kv_lora
    o4 = o3 + QK_ROPE
    wi = w_in[l]
    w_in_p = jnp.concatenate([wi[:, :o3], wi[:, o3:o4], jnp.zeros((d_model, HEAD_TILE - QK_ROPE), wi.dtype),
                              wi[:, o4:]], axis=1).astype(BF16)
    uq = w_uq[l].reshape(q_lora, N_HEADS, QK_HEAD)
    w_uq_p = _head_tile_cols(uq[..., QK_NOPE:], uq[..., :QK_NOPE]).astype(BF16)
    uk = w_uk[l]
    w_uk_p = _head_tile_cols(jnp.zeros((kv_lora, N_HEADS, QK_ROPE), uk.dtype), uk).astype(BF16)
    w_ukt = jnp.transpose(uk, (2, 1, 0)).reshape(QK_NOPE * N_HEADS, kv_lora).astype(BF16)
    uv = w_uv[l]
    w_uv_p = jnp.concatenate([uv, jnp.zeros((kv_lora, N_HEADS, HEAD_TILE - V_HEAD), uv.dtype)],
                             axis=-1).reshape(kv_lora, N_HEADS * HEAD_TILE).astype(BF16)
    lane = jnp.arange(HEAD_TILE)
    bound = (1.02 * LOG2E * math.sqrt(QK_HEAD)) * jnp.max(jnp.abs(q_norm_g[l])) * jnp.max(jnp.abs(k_norm_g[l]))
    q_one = jnp.where(lane == BIAS_LANE, 1.0, 0.0).astype(F32)[None]
    k_bias = jnp.where(lane == BIAS_LANE, -bound, 0.0).astype(F32)[None]
    v_one = jnp.tile(jnp.where(lane == ONES_LANE, 1.0, 0.0).astype(F32), N_HEADS)[None]
    ukt = jnp.transpose(uk, (1, 2, 0))
    w_abs = jnp.concatenate([jnp.zeros((N_HEADS, QK_ROPE, kv_lora), uk.dtype), ukt,
                             jnp.zeros((N_HEADS, HEAD_TILE - QK_HEAD, kv_lora), uk.dtype)], axis=1).astype(BF16)
    lane_ids = jnp.arange(HEAD_TILE)[None, :, None]
    col_ids = jnp.arange(N_HEADS * QK_ROPE)[None, None, :]
    head_ids = jnp.arange(N_HEADS)[:, None, None]
    w_sel = ((lane_ids < QK_ROPE) & (col_ids == head_ids * QK_ROPE + lane_ids)).astype(BF16)
    eye_h = jnp.eye(N_HEADS, dtype=uv.dtype)
    w_uv_bd = jnp.einsum("chd,hg->hcgd", uv, eye_h).reshape(N_HEADS * kv_lora, N_HEADS * V_HEAD).astype(BF16)
    g_k = _gain_tile(k_norm_g[l])
    return dict(
        g_mix=norm_mix_g[l][None].astype(F32), w_in=w_in_p, g_qa=q_a_norm_g[l][None].astype(F32), w_uq=w_uq_p,
        g_kva=kv_a_norm_g[l][None].astype(F32), g_q=_gain_tile(q_norm_g[l]), g_k=g_k,
        w_uk=w_uk_p, w_uv_p=w_uv_p, w_uv_bd=w_uv_bd, q_one=q_one, k_bias=k_bias, v_one=v_one, score_bound=bound,
        w_ukt=w_ukt, w_abs=w_abs, w_sel=w_sel, w_conv_out=w_conv_out[l].astype(BF16),
        w_o_mla=w_o_mla[l].astype(BF16),
        w_out=w_out[l].astype(BF16), g_ffn=norm_ffn_g[l][None].astype(F32), w_gate=w_gate[l].astype(BF16),
        w_up=w_up[l].astype(BF16), w_down=w_down[l].astype(BF16))


def kernel(x_prompt, x_sample, cache_kv_latent, cache_k_rope, state_conv, page_table, norm_mix_g, w_in,
           q_a_norm_g, w_uq, kv_a_norm_g, w_uk, w_uv, q_norm_g, k_norm_g, w_o_mla, conv_w, conv_b, conv_ln_g,
           conv_ln_b, w_conv_out, w_out, norm_ffn_g, w_gate, w_up, w_down):
    batch, seq, d_model = x_prompt.shape
    nb, t_new, _ = x_sample.shape
    depth = w_in.shape[0]
    n_pages = page_table.shape[1]
    past = n_pages * PAGE_SIZE
    n_state = state_conv.shape[2]
    d_conv = state_conv.shape[3]
    kv_lora = cache_kv_latent.shape[-1]

    tabs_p = _rope_tables(jnp.arange(seq))
    tabs_s = _rope_tables(jnp.tile(past + jnp.arange(t_new), nb))

    yp = x_prompt.reshape(batch * seq, d_model)
    ys = x_sample.reshape(nb * t_new, d_model)
    outs = [[] for _ in range(6)]
    for l in range(depth):
        wts = _layer_weights(l, norm_mix_g, w_in, q_a_norm_g, w_uq, kv_a_norm_g, w_uk, w_uv, q_norm_g,
                             k_norm_g, w_o_mla, w_conv_out, w_out, norm_ffn_g, w_gate, w_up, w_down)
        cb, lg, lb = (a[l][None].astype(F32) for a in (conv_b, conv_ln_g, conv_ln_b))

        u, lat, kpe, gates, q, k, v = _in_proj(yp, tabs_p, wts, tm=256, decode=False)
        attn = lax.cond(wts["score_bound"] <= FIXED_SHIFT_MAX_BOUND,
                        functools.partial(_flash_attention, batch=batch, tq=512, heads_per_step=N_HEADS,
                                          online=False),
                        functools.partial(_flash_attention, batch=batch, tq=512, heads_per_step=2,
                                          online=True), q, k, v)
        y_act = _conv_prompt(u, conv_w[l].astype(F32), cb, lg, lb, batch=batch, ts=512)
        yp = _out_ffn(yp, attn, y_act, gates, wts, tm=512, latent_attn=False)
        outs[0].append(lat.reshape(batch, seq, kv_lora))
        outs[1].append(kpe.reshape(batch, seq, QK_ROPE))
        outs[2].append(u.reshape(batch, seq, d_conv)[:, seq - n_state:])

        u, lat, kpe, gates, qa, qpe = _in_proj(ys, tabs_s, wts, tm=nb * t_new, decode=True)
        u_t = jnp.transpose(u.reshape(nb, t_new, d_conv), (1, 0, 2))
        state_t = jnp.transpose(state_conv[l], (1, 0, 2))
        y_act = _conv_decode(state_t, u_t, conv_w[l].astype(F32), cb, lg, lb)
        y_act = jnp.transpose(y_act, (1, 0, 2)).reshape(nb * t_new, d_conv)
        attn = _paged_attention(page_table, qa, qpe, lat, kpe, cache_kv_latent, jnp.swapaxes(cache_k_rope, 2, 3),
                                wts["w_ukt"], layer=l, tk=2048)
        ys = _out_ffn(ys, attn, y_act, gates, wts, tm=nb * t_new, latent_attn=True)
        outs[3].append(lat.reshape(nb, t_new, kv_lora))
        outs[4].append(kpe.reshape(nb, t_new, QK_ROPE))
        outs[5].append(jnp.transpose(jnp.concatenate([state_t[t_new:], u_t], axis=0), (1, 0, 2)))

    return (yp.reshape(batch, seq, d_model), ys.reshape(nb, t_new, d_model),
            *(jnp.stack(o) for o in outs))
```

```python
import functools
import math

import jax
import jax.numpy as jnp
from jax import lax
from jax.experimental import pallas as pl
from jax.experimental.pallas import tpu as pltpu

N_HEADS = 8
QK_NOPE = 64
QK_ROPE = 32
ROPE_HALF = QK_ROPE // 2
QK_HEAD = QK_NOPE + QK_ROPE
V_HEAD = 64
ROPE_THETA = 10000.0
EPS = 1e-6
PAGE_SIZE = 128

LANES = 128
HEAD_TILE = LANES
NOPE_OFF = QK_ROPE
VMEM_LIMIT_BYTES = 56 * 1024 * 1024

BIAS_LANE = QK_HEAD
ONES_LANE = V_HEAD
LOG2E = 1.4426950408889634
FIXED_SHIFT_MAX_BOUND = 40.0

F32 = jnp.float32
BF16 = jnp.bfloat16
NEG_BIG = -1e30


def _dot(a, b):
    return jnp.dot(a, b, preferred_element_type=F32)


def _dot_nt(a, b):
    return lax.dot_general(a, b, (((1,), (1,)), ((), ())), preferred_element_type=F32)


def _rms_scale(x):
    return lax.rsqrt(jnp.mean(x * x, axis=-1, keepdims=True) + EPS)


def _const_spec(shape, single_buffer=False):
    nd = len(shape)
    if single_buffer:
        return pl.BlockSpec(shape, lambda *_: (0,) * nd, pipeline_mode=pl.Buffered(1))
    return pl.BlockSpec(shape, lambda *_: (0,) * nd)


def _rope_tile(t, c_tab, sl_tab, sr_tab):
    return t * c_tab + pltpu.roll(t, ROPE_HALF, 1) * sr_tab + pltpu.roll(t, LANES - ROPE_HALF, 1) * sl_tab


def _in_proj_kernel(x_ref, ctab_ref, sltab_ref, srtab_ref, gmix_ref, w_in_ref, gqa_ref, w_uq_ref,
                    gkva_ref, gq_ref, gk_ref, qone_ref, kbias_ref, vone_ref, w_uk_ref, w_uv_ref,
                    wabs_ref, sel_ref, u_ref, lat_ref, kpe_ref, gates_ref, *attn_refs,
                    d_conv, q_lora, kv_lora, d_model, decode):
    x = x_ref[...]
    hb = (x * _rms_scale(x) * gmix_ref[...]).astype(BF16)
    c_tab, sl_tab, sr_tab = ctab_ref[...], sltab_ref[...], srtab_ref[...]

    o_cq = 2 * d_conv
    o_ckv = o_cq + q_lora
    o_kpe = o_ckv + kv_lora
    o_gate = o_kpe + HEAD_TILE

    ag = _dot(hb, w_in_ref[:, 0:o_cq])
    u_ref[...] = ag[:, :d_conv] * jax.nn.sigmoid(ag[:, d_conv:])

    cq = _dot(hb, w_in_ref[:, o_cq:o_ckv])
    cqn = (cq * _rms_scale(cq) * gqa_ref[...]).astype(BF16)
    q = _dot(cqn, w_uq_ref[...])
    q_scale = LOG2E / math.sqrt(QK_HEAD)
    qpe = None
    for h in range(N_HEADS):
        sl = slice(h * HEAD_TILE, (h + 1) * HEAD_TILE)
        qh = _rope_tile(q[:, sl], c_tab, sl_tab, sr_tab)
        ss = jnp.sum(qh * qh, axis=-1, keepdims=True)
        qh = qh * lax.rsqrt(ss * (1.0 / QK_HEAD) + EPS) * (gq_ref[...] * q_scale)
        if decode:
            qa_ref, qpe_ref = attn_refs
            qg = (qh * gk_ref[...]).astype(BF16)
            qa_ref[:, h * kv_lora:(h + 1) * kv_lora] = _dot(qg, wabs_ref[h]).astype(qa_ref.dtype)
            part = _dot(qg, sel_ref[h])
            qpe = part if qpe is None else qpe + part
        else:
            attn_refs[0][:, sl] = (qh + qone_ref[...]).astype(attn_refs[0].dtype)
    if decode:
        qpe_ref[...] = qpe.astype(qpe_ref.dtype)

    ck = _dot(hb, w_in_ref[:, o_ckv:o_gate])
    ckv = ck[:, :kv_lora]
    lat = ckv * _rms_scale(ckv) * gkva_ref[...]
    lat_ref[...] = lat
    kpe = _rope_tile(ck[:, kv_lora:], c_tab, sl_tab, sr_tab)
    kpe_ref[...] = kpe[:, :QK_ROPE]

    for g in range(2):
        sl = slice(g * d_model, (g + 1) * d_model)
        gate = jax.nn.sigmoid(_dot(hb, w_in_ref[:, o_gate + g * d_model:o_gate + (g + 1) * d_model]))
        gates_ref[:, sl] = gate.astype(gates_ref.dtype)

    if not decode:
        _, k_ref, v_ref = attn_refs
        latb = lat.astype(BF16)
        v_ref[...] = (_dot(latb, w_uv_ref[...]) + vone_ref[...]).astype(v_ref.dtype)
        kn = _dot(latb, w_uk_ref[...])
        for h in range(N_HEADS):
            sl = slice(h * HEAD_TILE, (h + 1) * HEAD_TILE)
            kh = kn[:, sl] + kpe
            ss = jnp.sum(kh * kh, axis=-1, keepdims=True)
            kh = kh * lax.rsqrt(ss * (1.0 / QK_HEAD) + EPS) * gk_ref[...] + kbias_ref[...]
            k_ref[:, sl] = kh.astype(k_ref.dtype)


def _in_proj(x, tabs, wts, *, tm, decode):
    t, d_model = x.shape
    n_tab_tiles = tabs[0].shape[0] // tm
    d_in_p = wts["w_in"].shape[1]
    q_lora = wts["w_uq"].shape[0]
    kv_lora = wts["w_uk"].shape[0]
    d_conv = (d_in_p - q_lora - kv_lora - HEAD_TILE - 2 * d_model) // 2
    qw = N_HEADS * HEAD_TILE
    vw = N_HEADS * HEAD_TILE

    row = lambda w: pl.BlockSpec((tm, w), lambda i: (i, 0))
    tab = pl.BlockSpec((tm, LANES), lambda i: (i % n_tab_tiles, 0))
    in_specs = [row(d_model), tab, tab, tab,
                _const_spec((1, d_model)), _const_spec(wts["w_in"].shape),
                _const_spec((1, q_lora)), _const_spec(wts["w_uq"].shape),
                _const_spec((1, kv_lora)), _const_spec((1, HEAD_TILE)), _const_spec((1, HEAD_TILE)),
                _const_spec((1, HEAD_TILE)), _const_spec((1, HEAD_TILE)), _const_spec((1, vw)),
                _const_spec(wts["w_uk"].shape), _const_spec(wts["w_uv_p"].shape),
                _const_spec(wts["w_abs"].shape), _const_spec(wts["w_sel"].shape)]
    out_shape = [jax.ShapeDtypeStruct((t, d_conv), F32), jax.ShapeDtypeStruct((t, kv_lora), F32),
                 jax.ShapeDtypeStruct((t, QK_ROPE), F32), jax.ShapeDtypeStruct((t, 2 * d_model), BF16)]
    out_specs = [row(d_conv), row(kv_lora), row(QK_ROPE), row(2 * d_model)]
    if decode:
        widths = [N_HEADS * kv_lora, N_HEADS * QK_ROPE]
    else:
        widths = [qw, qw, vw]
    out_shape += [jax.ShapeDtypeStruct((t, w), BF16) for w in widths]
    out_specs += [row(w) for w in widths]
    kern = functools.partial(_in_proj_kernel, d_conv=d_conv, q_lora=q_lora, kv_lora=kv_lora,
                             d_model=d_model, decode=decode)
    return pl.pallas_call(
        kern, grid=(t // tm,), in_specs=in_specs, out_specs=out_specs, out_shape=out_shape,
        compiler_params=pltpu.CompilerParams(dimension_semantics=("arbitrary",),
                                             vmem_limit_bytes=VMEM_LIMIT_BYTES),
        name="in_proj_decode" if decode else "in_proj_prompt",
    )(x, *tabs, wts["g_mix"], wts["w_in"], wts["g_qa"], wts["w_uq"], wts["g_kva"], wts["g_q"], wts["g_k"],
      wts["q_one"], wts["k_bias"], wts["v_one"], wts["w_uk"], wts["w_uv_p"], wts["w_abs"], wts["w_sel"])


def _ln_silu(y, g, b):
    mu = jnp.mean(y, axis=-1, keepdims=True)
    yc = y - mu
    yn = yc * lax.rsqrt(jnp.mean(yc * yc, axis=-1, keepdims=True) + EPS) * g + b
    return yn * jax.nn.sigmoid(yn)


HALO = 32


SUBLANES = 8


def _conv_prompt_kernel(cur_ref, prev_ref, w_ref, b_ref, lng_ref, lnb_ref, y_ref, ext_ref, sh_ref, *, rows):
    ts = cur_ref.shape[1]
    width = w_ref.shape[0]
    ext_ref[HALO:, :] = cur_ref[0]
    prev = prev_ref[0]
    ext_ref[0:HALO, :] = jnp.where(pl.program_id(1) > 0, prev, jnp.zeros_like(prev))
    span = sh_ref.shape[1]
    for r in range(1, SUBLANES):
        sh_ref[r - 1] = ext_ref[r:r + span, :]
    base = HALO - (width - 1)
    for r0 in range(0, ts, rows):
        acc = None
        for k in range(width):
            kk = base + k
            start = r0 + (kk // SUBLANES) * SUBLANES
            if kk % SUBLANES == 0:
                window = ext_ref[start:start + rows, :]
            else:
                window = sh_ref[kk % SUBLANES - 1, start:start + rows, :]
            term = window * w_ref[k:k + 1, :]
            acc = term if acc is None else acc + term
        y = _ln_silu(acc + b_ref[...], lng_ref[...], lnb_ref[...])
        y_ref[r0:r0 + rows, :] = y.astype(y_ref.dtype)


def _conv_prompt(u, conv_w, conv_b, ln_g, ln_b, *, batch, ts, rows=64):
    t, d_conv = u.shape
    seq = t // batch
    width = conv_w.shape[0]
    assert width - 1 <= HALO and ts % HALO == 0 and seq % ts == 0 and HALO % SUBLANES == 0
    u3 = u.reshape(batch, seq, d_conv)
    nst = seq // ts
    span = ts + HALO - SUBLANES
    kern = functools.partial(_conv_prompt_kernel, rows=rows)
    return pl.pallas_call(
        kern, grid=(batch, nst),
        in_specs=[pl.BlockSpec((1, ts, d_conv), lambda b, i: (b, i, 0)),
                  pl.BlockSpec((1, HALO, d_conv), lambda b, i: (b, jnp.maximum(i * (ts // HALO) - 1, 0), 0)),
                  _const_spec(conv_w.shape), _const_spec((1, d_conv)), _const_spec((1, d_conv)),
                  _const_spec((1, d_conv))],
        out_specs=pl.BlockSpec((ts, d_conv), lambda b, i: (b * nst + i, 0)),
        out_shape=jax.ShapeDtypeStruct((t, d_conv), BF16),
        scratch_shapes=[pltpu.VMEM((ts + HALO, d_conv), F32), pltpu.VMEM((SUBLANES - 1, span, d_conv), F32)],
        compiler_params=pltpu.CompilerParams(dimension_semantics=("arbitrary", "arbitrary"),
                                             vmem_limit_bytes=VMEM_LIMIT_BYTES),
        name="conv_prompt",
    )(u3, u3, conv_w, conv_b, ln_g, ln_b)


def _conv_decode_kernel(state_ref, unew_ref, w_ref, b_ref, lng_ref, lnb_ref, y_ref):
    n_state = state_ref.shape[0]
    for t in range(unew_ref.shape[0]):
        acc = None
        for k in range(w_ref.shape[0]):
            j = t + k
            src = state_ref[j] if j < n_state else unew_ref[j - n_state]
            term = src * w_ref[k:k + 1, :]
            acc = term if acc is None else acc + term
        y_ref[t] = _ln_silu(acc + b_ref[...], lng_ref[...], lnb_ref[...]).astype(y_ref.dtype)


def _conv_decode(state_t, u_new_t, conv_w, conv_b, ln_g, ln_b):
    t_new, nb, d_conv = u_new_t.shape
    assert state_t.shape[0] == conv_w.shape[0] - 1
    full = lambda a: _const_spec(a.shape)
    return pl.pallas_call(
        _conv_decode_kernel, grid=(1,),
        in_specs=[full(state_t), full(u_new_t), full(conv_w), _const_spec((1, d_conv)),
                  _const_spec((1, d_conv)), _const_spec((1, d_conv))],
        out_specs=_const_spec((t_new, nb, d_conv)),
        out_shape=jax.ShapeDtypeStruct((t_new, nb, d_conv), BF16),
        compiler_params=pltpu.CompilerParams(dimension_semantics=("arbitrary",),
                                             vmem_limit_bytes=VMEM_LIMIT_BYTES),
        name="conv_decode",
    )(state_t, u_new_t, conv_w, conv_b, ln_g, ln_b)


def _flash_kernel(qi_tab, ki_tab, q_ref, k_ref, v_ref, o_ref, acc_ref, *m_refs, online):
    step = pl.program_id(2)
    qi = qi_tab[step]
    ki = ki_tab[step]
    tq = q_ref.shape[1]
    tk = k_ref.shape[1]
    n_heads = acc_ref.shape[0]

    @pl.when(ki == 0)
    def _():
        acc_ref[...] = jnp.zeros_like(acc_ref)
        if online:
            m_refs[0][...] = jnp.full_like(m_refs[0], NEG_BIG)

    def accumulate(masked):
        if masked:
            visible = (lax.broadcasted_iota(jnp.int32, (tq, tk), 1)
                       <= lax.broadcasted_iota(jnp.int32, (tq, tk), 0))
        for hh in range(n_heads):
            sl = slice(hh * HEAD_TILE, (hh + 1) * HEAD_TILE)
            s = _dot_nt(q_ref[0, :, sl], k_ref[0, :, sl])
            if masked:
                s = jnp.where(visible, s, NEG_BIG)
            if online:
                m_old = m_refs[0][hh]
                m_new = jnp.maximum(m_old, jnp.max(s, axis=-1, keepdims=True))
                m_refs[0][hh] = m_new
                p = jnp.exp2(s - m_new).astype(BF16)
                acc_ref[hh] = jnp.exp2(m_old - m_new) * acc_ref[hh] + _dot(p, v_ref[0, :, sl])
            else:
                acc_ref[hh] += _dot(jnp.exp2(s).astype(BF16), v_ref[0, :, sl])

    @pl.when(ki < qi)
    def _():
        accumulate(False)

    @pl.when(ki == qi)
    def _():
        accumulate(True)
        lane = lax.broadcasted_iota(jnp.int32, (tq, HEAD_TILE), 1)
        for pair in range(n_heads // 2):
            a0, a1 = acc_ref[2 * pair], acc_ref[2 * pair + 1]
            h0 = a0 / a0[:, ONES_LANE:ONES_LANE + 1]
            h1 = a1 / a1[:, ONES_LANE:ONES_LANE + 1]
            o_ref[0, :, pair * HEAD_TILE:(pair + 1) * HEAD_TILE] = jnp.where(
                lane < V_HEAD, h0, pltpu.roll(h1, V_HEAD, 1)).astype(o_ref.dtype)


def _flash_attention(q, k, v, *, batch, tq, heads_per_step, online):
    t = q.shape[0]
    seq = t // batch
    nq = seq // tq
    assert 2 * V_HEAD == HEAD_TILE and heads_per_step % 2 == 0 and N_HEADS % heads_per_step == 0
    q3, k3, v3 = (a.reshape(batch, seq, a.shape[1]) for a in (q, k, v))
    qi_list = [qi for qi in range(nq) for _ in range(qi + 1)]
    ki_list = [ki for qi in range(nq) for ki in range(qi + 1)]
    qi_tab = jnp.asarray(qi_list, jnp.int32)
    ki_tab = jnp.asarray(ki_list, jnp.int32)
    pair_tile = lambda tab: pl.BlockSpec((1, tq, heads_per_step * HEAD_TILE),
                                         lambda b, hp, s, qt, kt: (b, (qt if tab == "q" else kt)[s], hp))
    scratch = [pltpu.VMEM((heads_per_step, tq, HEAD_TILE), F32)]
    if online:
        scratch.append(pltpu.VMEM((heads_per_step, tq, 1), F32))
    grid_spec = pltpu.PrefetchScalarGridSpec(
        num_scalar_prefetch=2, grid=(batch, N_HEADS // heads_per_step, len(qi_list)),
        in_specs=[pair_tile("q"), pair_tile("k"), pair_tile("k")],
        out_specs=pl.BlockSpec((1, tq, heads_per_step * V_HEAD), lambda b, hp, s, qt, kt: (b, qt[s], hp)),
        scratch_shapes=scratch)
    out = pl.pallas_call(
        functools.partial(_flash_kernel, online=online), grid_spec=grid_spec,
        out_shape=jax.ShapeDtypeStruct((batch, seq, N_HEADS * V_HEAD), BF16),
        compiler_params=pltpu.CompilerParams(dimension_semantics=("arbitrary",) * 3,
                                             vmem_limit_bytes=VMEM_LIMIT_BYTES),
        name="flash_prompt_online" if online else "flash_prompt",
    )(qi_tab, ki_tab, q3, k3, v3)
    return out.reshape(t, N_HEADS * V_HEAD)


FETCH_SLOTS = 3
CHUNK_UNROLL = 1


def _paged_kernel(pt_ref, qa_ref, qpe_ref, latn_ref, kpen_ref, wukt_ref, lat_hbm, kpet_hbm,
                  o_ref, lat_buf, kpet_buf, sems, cb_ref, cbt_ref, cbn_ref, kpn_ref, s_ref, lhs_ref, *,
                  layer, n_pages, tk):
    b = pl.program_id(0)
    nb = pl.num_programs(0)
    past = n_pages * PAGE_SIZE
    n_chunks = past // tk
    t_new = latn_ref.shape[1]
    rows = qa_ref.shape[1]

    def page_copies(seq, slot, j):
        page = pt_ref[seq * n_pages + j]
        dst = pl.ds(j * PAGE_SIZE, PAGE_SIZE)
        return (pltpu.make_async_copy(lat_hbm.at[layer, page], lat_buf.at[slot, dst], sems.at[slot, 0]),
                pltpu.make_async_copy(kpet_hbm.at[layer, page], kpet_buf.at[slot, j], sems.at[slot, 1]))

    def start_fetch(seq, slot, pages=None):
        for j in (range(n_pages) if pages is None else pages):
            lat_cp, kpe_cp = page_copies(seq, slot, j)
            lat_cp.start()
            kpe_cp.start(priority=1)

    def wait_fetch(seq, slot):
        for j in range(n_pages):
            for cp in page_copies(seq, slot, j):
                cp.wait()

    slot = b % FETCH_SLOTS

    @pl.when(b == 0)
    def _():
        for s in range(FETCH_SLOTS - 1):
            start_fetch(jnp.minimum(s, nb - 1), s)
        lhs_ref[0:N_HEADS * QK_NOPE, :] = wukt_ref[...]
        cbn_ref[...] = jnp.zeros_like(cbn_ref)
        kpn_ref[...] = jnp.zeros_like(kpn_ref)

    lhs_ref[N_HEADS * QK_NOPE:, :] = qa_ref[0]
    qpe = qpe_ref[0]
    ones = jnp.ones((N_HEADS, QK_ROPE), BF16)
    pages_per_chunk = tk // PAGE_SIZE

    def scores(cbt, ss_pe, s_pe):
        r = _dot(lhs_ref[...], cbt)
        n = cbt.shape[1]
        kn = r[:N_HEADS * QK_NOPE]
        ss = jnp.sum((kn * kn).reshape(QK_NOPE, N_HEADS, n), axis=0) + ss_pe
        inv = lax.rsqrt(ss * (1.0 / QK_HEAD) + EPS)
        return (r[N_HEADS * QK_NOPE:] + s_pe) * jnp.concatenate([inv] * t_new, axis=0)

    wait_fetch(b, slot)

    def lane_tiles(a):
        return [a[:, i:i + LANES] for i in range(0, a.shape[1], LANES)]

    def chunk_group(g, m):
        for j in range(CHUNK_UNROLL):
            c = g * CHUNK_UNROLL + j
            st = pl.multiple_of(c * tk, tk)
            cb = lat_buf[slot, pl.ds(st, tk), :].astype(BF16)
            cb_ref[pl.ds(st, tk), :] = cb
            kpt = jnp.concatenate([kpet_buf[slot, c * pages_per_chunk + i] for i in range(pages_per_chunk)],
                                  axis=1)
            cbt_ref[j] = cb.T
            sc = scores(cbt_ref[j], jnp.sum(kpt * kpt, axis=0, keepdims=True), _dot(qpe, kpt.astype(BF16)))
            s_ref[c] = sc
            m = functools.reduce(jnp.maximum, [m] + lane_tiles(sc))
        return m

    m = lax.fori_loop(0, n_chunks // CHUNK_UNROLL, chunk_group, jnp.full((rows, LANES), NEG_BIG, F32))
    m = jnp.max(m, axis=-1, keepdims=True)

    cbn_ref[0:t_new, :] = latn_ref[0].astype(BF16)
    kpn_ref[0:t_new, :] = kpen_ref[0]
    cbn = cbn_ref[...]
    kpn = kpn_ref[...]
    sq = kpn * kpn
    hi = sq.astype(BF16)
    lo = (sq - hi.astype(F32)).astype(BF16)
    col = lax.broadcasted_iota(jnp.int32, (rows, LANES), 1)
    qrow = lax.broadcasted_iota(jnp.int32, (rows, LANES), 0) // N_HEADS
    s_new = scores(cbn.astype(F32).T.astype(BF16), _dot_nt(ones, hi) + _dot_nt(ones, lo),
                   _dot_nt(qpe, kpn.astype(BF16)))
    s_new = jnp.where(col <= qrow, s_new, NEG_BIG)
    m_new = jnp.max(s_new, axis=-1, keepdims=True)
    p_new = jnp.exp2(s_new - m_new)
    l_new = jnp.sum(p_new, axis=-1, keepdims=True)
    o_new = _dot(p_new.astype(BF16), cbn)

    ahead = FETCH_SLOTS - 1
    nxt = jnp.minimum(b + ahead, nb - 1)
    nxt_slot = (b + ahead) % FETCH_SLOTS
    pages_per_pv = n_pages // n_chunks
    l = jnp.zeros((rows, LANES), F32)
    o = jnp.zeros((rows, cb_ref.shape[1]), F32)
    for c in range(n_chunks):
        start_fetch(nxt, nxt_slot, range(c * pages_per_pv, (c + 1) * pages_per_pv))
        p = jnp.exp2(s_ref[c] - m)
        l = functools.reduce(jnp.add, [l] + lane_tiles(p))
        o = o + _dot(p.astype(BF16), cb_ref[c * tk:(c + 1) * tk, :])
    l = jnp.sum(l, axis=-1, keepdims=True)

    m_all = jnp.maximum(m, m_new)
    w_past = jnp.exp2(m - m_all)
    w_new = jnp.exp2(m_new - m_all)
    o_ref[0] = ((o * w_past + o_new * w_new) / (l * w_past + l_new * w_new)).astype(o_ref.dtype)

    @pl.when(b == nb - 1)
    def _():
        for d in range(1, FETCH_SLOTS):
            wait_fetch(nxt, (b + d) % FETCH_SLOTS)


def _paged_attention(page_table, qa, qpe, lat_new, kpe_new, cache_lat, cache_kpe_t, wukt, *, layer, tk):
    nb, n_pages = page_table.shape
    assert tk % PAGE_SIZE == 0 and cache_kpe_t.shape[2:] == (QK_ROPE, PAGE_SIZE)
    assert (n_pages * PAGE_SIZE) % (tk * CHUNK_UNROLL) == 0
    t_new = qa.shape[0] // nb
    kv_lora = cache_lat.shape[-1]
    past = n_pages * PAGE_SIZE
    rows = t_new * N_HEADS
    qa3 = qa.reshape(nb, rows, kv_lora)
    qpe3 = qpe.reshape(nb, rows, QK_ROPE)
    ln3 = lat_new.reshape(nb, t_new, kv_lora)
    kn3 = kpe_new.reshape(nb, t_new, QK_ROPE)
    per_seq = lambda r, w: pl.BlockSpec((1, r, w), lambda b, pt: (b, 0, 0))
    const = lambda a: pl.BlockSpec(a.shape, lambda b, pt: (0,) * a.ndim)
    grid_spec = pltpu.PrefetchScalarGridSpec(
        num_scalar_prefetch=1, grid=(nb,),
        in_specs=[per_seq(rows, kv_lora), per_seq(rows, QK_ROPE), per_seq(t_new, kv_lora),
                  per_seq(t_new, QK_ROPE), const(wukt), pl.BlockSpec(memory_space=pl.ANY),
                  pl.BlockSpec(memory_space=pl.ANY)],
        out_specs=per_seq(rows, kv_lora),
        scratch_shapes=[pltpu.VMEM((FETCH_SLOTS, past, kv_lora), F32),
                        pltpu.VMEM((FETCH_SLOTS, n_pages, QK_ROPE, PAGE_SIZE), F32),
                        pltpu.SemaphoreType.DMA((FETCH_SLOTS, 2)),
                        pltpu.VMEM((past, kv_lora), BF16), pltpu.VMEM((CHUNK_UNROLL, kv_lora, tk), BF16),
                        pltpu.VMEM((LANES, kv_lora), BF16),
                        pltpu.VMEM((LANES, QK_ROPE), F32), pltpu.VMEM((past // tk, rows, tk), F32),
                        pltpu.VMEM((N_HEADS * QK_NOPE + rows, kv_lora), BF16)])
    kern = functools.partial(_paged_kernel, layer=layer, n_pages=n_pages, tk=tk)
    out = pl.pallas_call(
        kern, grid_spec=grid_spec,
        out_shape=jax.ShapeDtypeStruct((nb, rows, kv_lora), BF16),
        compiler_params=pltpu.CompilerParams(dimension_semantics=("arbitrary",),
                                             vmem_limit_bytes=VMEM_LIMIT_BYTES),
        name="paged_decode",
    )(page_table.reshape(-1), qa3, qpe3, ln3, kn3, wukt, cache_lat, cache_kpe_t)
    return out.reshape(nb * t_new, N_HEADS * kv_lora)


FF_CHUNK = 256


def _out_ffn_kernel(x_ref, attn_ref, y_ref, gates_ref, wco_ref, womla_ref, wout_ref, gffn_ref,
                    wg_ref, wu_ref, wd_ref, *rest, latent_attn):
    if latent_attn:
        wuv_ref, o_ref, act_ref = rest
        attn = _dot(attn_ref[...], wuv_ref[...]).astype(BF16)
    else:
        o_ref, act_ref = rest
        attn = attn_ref[...]
    y_cur = y_ref[...]
    d_model = x_ref.shape[1]
    d_ff = wg_ref.shape[1]
    half = d_model // 2
    for c0 in (0, half):
        sl = slice(c0, c0 + half)
        merged = (gates_ref[:, sl] * _dot(y_cur, wco_ref[:, sl])
                  + gates_ref[:, d_model + c0:d_model + c0 + half] * _dot(attn, womla_ref[:, sl]))
        act_ref[:, sl] = merged.astype(BF16)
    o_ref[...] = x_ref[...] + _dot(act_ref[:, :d_model], wout_ref[...])
    x1 = o_ref[...]
    h2 = (x1 * _rms_scale(x1) * gffn_ref[...]).astype(BF16)
    for c0 in range(0, d_ff, FF_CHUNK):
        sl = slice(c0, c0 + FF_CHUNK)
        g = _dot(h2, wg_ref[:, sl])
        act_ref[:, sl] = (g * jax.nn.sigmoid(g) * _dot(h2, wu_ref[:, sl])).astype(BF16)
    o_ref[...] += _dot(act_ref[...], wd_ref[...])


def _out_ffn(x, attn, y_act, gates, wts, *, tm, latent_attn):
    t, d_model = x.shape
    d_ff = wts["w_gate"].shape[1]
    assert d_ff % FF_CHUNK == 0 and d_ff >= d_model
    row = lambda w: pl.BlockSpec((tm, w), lambda i: (i, 0))
    wspec = lambda a: _const_spec(a.shape, single_buffer=True)
    weights = [wts["w_conv_out"], wts["w_o_mla"], wts["w_out"], wts["g_ffn"], wts["w_gate"], wts["w_up"],
               wts["w_down"]] + ([wts["w_uv_bd"]] if latent_attn else [])
    return pl.pallas_call(
        functools.partial(_out_ffn_kernel, latent_attn=latent_attn), grid=(t // tm,),
        in_specs=[row(d_model), row(attn.shape[1]), row(y_act.shape[1]), row(2 * d_model)]
                 + [wspec(w) for w in weights],
        out_specs=row(d_model),
        out_shape=jax.ShapeDtypeStruct((t, d_model), F32),
        scratch_shapes=[pltpu.VMEM((tm, d_ff), BF16)],
        compiler_params=pltpu.CompilerParams(dimension_semantics=("arbitrary",),
                                             vmem_limit_bytes=VMEM_LIMIT_BYTES),
        name="out_ffn_decode" if latent_attn else "out_ffn_prompt",
    )(x, attn, y_act, gates, *weights)


def _head_tile_cols(w_rope, w_nope):
    lead = w_nope.shape[:-2]
    pad = jnp.zeros(lead + (N_HEADS, HEAD_TILE - QK_HEAD), w_nope.dtype)
    tile = jnp.concatenate([w_rope, w_nope, pad], axis=-1)
    return tile.reshape(lead + (N_HEADS * HEAD_TILE,))


def _gain_tile(g_compact):
    g = g_compact.astype(F32)
    return jnp.concatenate([g[QK_NOPE:], g[QK_NOPE:], g[:QK_NOPE], jnp.zeros((HEAD_TILE - QK_HEAD,), F32)])[None]


def _rope_tables(pos):
    inv_freq = 1.0 / (ROPE_THETA ** (jnp.arange(0, QK_ROPE, 2, dtype=F32) / QK_ROPE))
    ang = pos.astype(F32)[:, None] * inv_freq[None, :]
    c, s = jnp.cos(ang), jnp.sin(ang)
    n = pos.shape[0]
    z16 = jnp.zeros((n, ROPE_HALF), F32)
    rest = LANES - QK_ROPE
    c_tab = jnp.concatenate([c, c, jnp.ones((n, QK_NOPE), F32), jnp.zeros((n, rest - QK_NOPE), F32)], axis=1)
    sl_tab = jnp.concatenate([-s, z16, jnp.zeros((n, rest), F32)], axis=1)
    sr_tab = jnp.concatenate([z16, s, jnp.zeros((n, rest), F32)], axis=1)
    return c_tab, sl_tab, sr_tab


def _layer_weights(l, norm_mix_g, w_in, q_a_norm_g, w_uq, kv_a_norm_g, w_uk, w_uv, q_norm_g, k_norm_g,
                   w_o_mla, w_conv_out, w_out, norm_ffn_g, w_gate, w_up, w_down):
    d_model = w_in.shape[1]
    kv_lora = w_uk.shape[1]
    q_lora = w_uq.shape[1]
    d_conv = w_conv_out.shape[1]
    o1 = 2 * d_conv
    o2 = o1 + q_lora
    o3 = o2 + kv_lora
    o4 = o3 + QK_ROPE
    wi = w_in[l]
    w_in_p = jnp.concatenate([wi[:, :o3], wi[:, o3:o4], jnp.zeros((d_model, HEAD_TILE - QK_ROPE), wi.dtype),
                              wi[:, o4:]], axis=1).astype(BF16)
    uq = w_uq[l].reshape(q_lora, N_HEADS, QK_HEAD)
    w_uq_p = _head_tile_cols(uq[..., QK_NOPE:], uq[..., :QK_NOPE]).astype(BF16)
    uk = w_uk[l]
    w_uk_p = _head_tile_cols(jnp.zeros((kv_lora, N_HEADS, QK_ROPE), uk.dtype), uk).astype(BF16)
    w_ukt = jnp.transpose(uk, (2, 1, 0)).reshape(QK_NOPE * N_HEADS, kv_lora).astype(BF16)
    uv = w_uv[l]
    w_uv_p = jnp.concatenate([uv, jnp.zeros((kv_lora, N_HEADS, HEAD_TILE - V_HEAD), uv.dtype)],
                             axis=-1).reshape(kv_lora, N_HEADS * HEAD_TILE).astype(BF16)
    lane = jnp.arange(HEAD_TILE)
    bound = (1.02 * LOG2E * math.sqrt(QK_HEAD)) * jnp.max(jnp.abs(q_norm_g[l])) * jnp.max(jnp.abs(k_norm_g[l]))
    q_one = jnp.where(lane == BIAS_LANE, 1.0, 0.0).astype(F32)[None]
    k_bias = jnp.where(lane == BIAS_LANE, -bound, 0.0).astype(F32)[None]
    v_one = jnp.tile(jnp.where(lane == ONES_LANE, 1.0, 0.0).astype(F32), N_HEADS)[None]
    ukt = jnp.transpose(uk, (1, 2, 0))
    w_abs = jnp.concatenate([jnp.zeros((N_HEADS, QK_ROPE, kv_lora), uk.dtype), ukt,
                             jnp.zeros((N_HEADS, HEAD_TILE - QK_HEAD, kv_lora), uk.dtype)], axis=1).astype(BF16)
    lane_ids = jnp.arange(HEAD_TILE)[None, :, None]
    col_ids = jnp.arange(N_HEADS * QK_ROPE)[None, None, :]
    head_ids = jnp.arange(N_HEADS)[:, None, None]
    w_sel = ((lane_ids < QK_ROPE) & (col_ids == head_ids * QK_ROPE + lane_ids)).astype(BF16)
    eye_h = jnp.eye(N_HEADS, dtype=uv.dtype)
    w_uv_bd = jnp.einsum("chd,hg->hcgd", uv, eye_h).reshape(N_HEADS * kv_lora, N_HEADS * V_HEAD).astype(BF16)
    g_k = _gain_tile(k_norm_g[l])
    return dict(
        g_mix=norm_mix_g[l][None].astype(F32), w_in=w_in_p, g_qa=q_a_norm_g[l][None].astype(F32), w_uq=w_uq_p,
        g_kva=kv_a_norm_g[l][None].astype(F32), g_q=_gain_tile(q_norm_g[l]), g_k=g_k,
        w_uk=w_uk_p, w_uv_p=w_uv_p, w_uv_bd=w_uv_bd, q_one=q_one, k_bias=k_bias, v_one=v_one, score_bound=bound,
        w_ukt=w_ukt, w_abs=w_abs, w_sel=w_sel, w_conv_out=w_conv_out[l].astype(BF16),
        w_o_mla=w_o_mla[l].astype(BF16),
        w_out=w_out[l].astype(BF16), g_ffn=norm_ffn_g[l][None].astype(F32), w_gate=w_gate[l].astype(BF16),
        w_up=w_up[l].astype(BF16), w_down=w_down[l].astype(BF16))


def kernel(x_prompt, x_sample, cache_kv_latent, cache_k_rope, state_conv, page_table, norm_mix_g, w_in,
           q_a_norm_g, w_uq, kv_a_norm_g, w_uk, w_uv, q_norm_g, k_norm_g, w_o_mla, conv_w, conv_b, conv_ln_g,
           conv_ln_b, w_conv_out, w_out, norm_ffn_g, w_gate, w_up, w_down):
    batch, seq, d_model = x_prompt.shape
    nb, t_new, _ = x_sample.shape
    depth = w_in.shape[0]
    n_pages = page_table.shape[1]
    past = n_pages * PAGE_SIZE
    n_state = state_conv.shape[2]
    d_conv = state_conv.shape[3]
    kv_lora = cache_kv_latent.shape[-1]

    tabs_p = _rope_tables(jnp.arange(seq))
    tabs_s = _rope_tables(jnp.tile(past + jnp.arange(t_new), nb))

    yp = x_prompt.reshape(batch * seq, d_model)
    ys = x_sample.reshape(nb * t_new, d_model)
    outs = [[] for _ in range(6)]
    for l in range(depth):
        wts = _layer_weights(l, norm_mix_g, w_in, q_a_norm_g, w_uq, kv_a_norm_g, w_uk, w_uv, q_norm_g,
                             k_norm_g, w_o_mla, w_conv_out, w_out, norm_ffn_g, w_gate, w_up, w_down)
        cb, lg, lb = (a[l][None].astype(F32) for a in (conv_b, conv_ln_g, conv_ln_b))

        u, lat, kpe, gates, q, k, v = _in_proj(yp, tabs_p, wts, tm=256, decode=False)
        attn = lax.cond(wts["score_bound"] <= FIXED_SHIFT_MAX_BOUND,
                        functools.partial(_flash_attention, batch=batch, tq=512, heads_per_step=N_HEADS,
                                          online=False),
                        functools.partial(_flash_attention, batch=batch, tq=512, heads_per_step=2,
                                          online=True), q, k, v)
        y_act = _conv_prompt(u, conv_w[l].astype(F32), cb, lg, lb, batch=batch, ts=512)
        yp = _out_ffn(yp, attn, y_act, gates, wts, tm=512, latent_attn=False)
        outs[0].append(lat.reshape(batch, seq, kv_lora))
        outs[1].append(kpe.reshape(batch, seq, QK_ROPE))
        outs[2].append(u.reshape(batch, seq, d_conv)[:, seq - n_state:])

        u, lat, kpe, gates, qa, qpe = _in_proj(ys, tabs_s, wts, tm=nb * t_new, decode=True)
        u_t = jnp.transpose(u.reshape(nb, t_new, d_conv), (1, 0, 2))
        state_t = jnp.transpose(state_conv[l], (1, 0, 2))
        y_act = _conv_decode(state_t, u_t, conv_w[l].astype(F32), cb, lg, lb)
        y_act = jnp.transpose(y_act, (1, 0, 2)).reshape(nb * t_new, d_conv)
        attn = _paged_attention(page_table, qa, qpe, lat, kpe, cache_kv_latent, jnp.swapaxes(cache_k_rope, 2, 3),
                                wts["w_ukt"], layer=l, tk=4096)
        ys = _out_ffn(ys, attn, y_act, gates, wts, tm=nb * t_new, latent_attn=True)
        outs[3].append(lat.reshape(nb, t_new, kv_lora))
        outs[4].append(kpe.reshape(nb, t_new, QK_ROPE))
        outs[5].append(jnp.transpose(jnp.concatenate([state_t[t_new:], u_t], axis=0), (1, 0, 2)))

    return (yp.reshape(batch, seq, d_model), ys.reshape(nb, t_new, d_model),
            *(jnp.stack(o) for o in outs))
```

```python
import functools
import math

import jax
import jax.numpy as jnp
from jax import lax
from jax.experimental import pallas as pl
from jax.experimental.pallas import tpu as pltpu

N_HEADS = 8
QK_NOPE = 64
QK_ROPE = 32
ROPE_HALF = QK_ROPE // 2
QK_HEAD = QK_NOPE + QK_ROPE
V_HEAD = 64
ROPE_THETA = 10000.0
EPS = 1e-6
PAGE_SIZE = 128

LANES = 128
HEAD_TILE = LANES
NOPE_OFF = QK_ROPE
VMEM_LIMIT_BYTES = 56 * 1024 * 1024

BIAS_LANE = QK_HEAD
ONES_LANE = V_HEAD
LOG2E = 1.4426950408889634
FIXED_SHIFT_MAX_BOUND = 40.0

F32 = jnp.float32
BF16 = jnp.bfloat16
NEG_BIG = -1e30


def _dot(a, b):
    return jnp.dot(a, b, preferred_element_type=F32)


def _dot_nt(a, b):
    return lax.dot_general(a, b, (((1,), (1,)), ((), ())), preferred_element_type=F32)


def _rms_scale(x):
    return lax.rsqrt(jnp.mean(x * x, axis=-1, keepdims=True) + EPS)


def _const_spec(shape, single_buffer=False):
    nd = len(shape)
    if single_buffer:
        return pl.BlockSpec(shape, lambda *_: (0,) * nd, pipeline_mode=pl.Buffered(1))
    return pl.BlockSpec(shape, lambda *_: (0,) * nd)


def _rope_tile(t, c_tab, sl_tab, sr_tab):
    return t * c_tab + pltpu.roll(t, ROPE_HALF, 1) * sr_tab + pltpu.roll(t, LANES - ROPE_HALF, 1) * sl_tab


N_ROW_INPUTS = 4
N_WEIGHT_INPUTS = 14


def _in_proj_kernel(*refs, sub_rows, **kw):
    n_in = N_ROW_INPUTS + N_WEIGHT_INPUTS
    rows, weights, outs = refs[:N_ROW_INPUTS], refs[N_ROW_INPUTS:n_in], refs[n_in:]
    for r0 in range(0, rows[0].shape[0], sub_rows):
        view = lambda ref: ref.at[pl.ds(r0, sub_rows)]
        _in_proj_rows(*map(view, rows), *weights, *map(view, outs), **kw)


def _in_proj_rows(x_ref, ctab_ref, sltab_ref, srtab_ref, gmix_ref, w_in_ref, gqa_ref, w_uq_ref,
                  gkva_ref, gq_ref, gk_ref, qone_ref, kbias_ref, vone_ref, w_uk_ref, w_uv_ref,
                  wabs_ref, sel_ref, u_ref, lat_ref, kpe_ref, gates_ref, *attn_refs,
                  d_conv, q_lora, kv_lora, d_model, decode):
    x = x_ref[...]
    hb = (x * _rms_scale(x) * gmix_ref[...]).astype(BF16)
    c_tab, sl_tab, sr_tab = ctab_ref[...], sltab_ref[...], srtab_ref[...]

    o_cq = 2 * d_conv
    o_ckv = o_cq + q_lora
    o_kpe = o_ckv + kv_lora
    o_gate = o_kpe + HEAD_TILE

    ag = _dot(hb, w_in_ref[:, 0:o_cq])
    u_ref[...] = ag[:, :d_conv] * jax.nn.sigmoid(ag[:, d_conv:])

    cq = _dot(hb, w_in_ref[:, o_cq:o_ckv])
    cqn = (cq * _rms_scale(cq) * gqa_ref[...]).astype(BF16)
    q = _dot(cqn, w_uq_ref[...])
    q_scale = LOG2E / math.sqrt(QK_HEAD)
    qpe = None
    for h in range(N_HEADS):
        sl = slice(h * HEAD_TILE, (h + 1) * HEAD_TILE)
        qh = _rope_tile(q[:, sl], c_tab, sl_tab, sr_tab)
        ss = jnp.sum(qh * qh, axis=-1, keepdims=True)
        qh = qh * lax.rsqrt(ss * (1.0 / QK_HEAD) + EPS) * (gq_ref[...] * q_scale)
        if decode:
            qa_ref, qpe_ref = attn_refs
            qg = (qh * gk_ref[...]).astype(BF16)
            qa_ref[:, h * kv_lora:(h + 1) * kv_lora] = _dot(qg, wabs_ref[h]).astype(qa_ref.dtype)
            part = _dot(qg, sel_ref[h])
            qpe = part if qpe is None else qpe + part
        else:
            attn_refs[0][:, sl] = (qh + qone_ref[...]).astype(attn_refs[0].dtype)
    if decode:
        qpe_ref[...] = qpe.astype(qpe_ref.dtype)

    ck = _dot(hb, w_in_ref[:, o_ckv:o_gate])
    ckv = ck[:, :kv_lora]
    lat = ckv * _rms_scale(ckv) * gkva_ref[...]
    lat_ref[...] = lat
    kpe = _rope_tile(ck[:, kv_lora:], c_tab, sl_tab, sr_tab)
    kpe_ref[...] = kpe[:, :QK_ROPE]

    for g in range(2):
        sl = slice(g * d_model, (g + 1) * d_model)
        gate = jax.nn.sigmoid(_dot(hb, w_in_ref[:, o_gate + g * d_model:o_gate + (g + 1) * d_model]))
        gates_ref[:, sl] = gate.astype(gates_ref.dtype)

    if not decode:
        _, k_ref, v_ref = attn_refs
        latb = lat.astype(BF16)
        v_ref[...] = (_dot(latb, w_uv_ref[...]) + vone_ref[...]).astype(v_ref.dtype)
        kn = _dot(latb, w_uk_ref[...])
        for h in range(N_HEADS):
            sl = slice(h * HEAD_TILE, (h + 1) * HEAD_TILE)
            kh = kn[:, sl] + kpe
            ss = jnp.sum(kh * kh, axis=-1, keepdims=True)
            kh = kh * lax.rsqrt(ss * (1.0 / QK_HEAD) + EPS) * gk_ref[...] + kbias_ref[...]
            k_ref[:, sl] = kh.astype(k_ref.dtype)


def _in_proj(x, tabs, wts, *, tm, sub_rows, decode):
    t, d_model = x.shape
    n_tab_tiles = tabs[0].shape[0] // tm
    d_in_p = wts["w_in"].shape[1]
    q_lora = wts["w_uq"].shape[0]
    kv_lora = wts["w_uk"].shape[0]
    d_conv = (d_in_p - q_lora - kv_lora - HEAD_TILE - 2 * d_model) // 2
    qw = N_HEADS * HEAD_TILE
    vw = N_HEADS * HEAD_TILE

    row = lambda w: pl.BlockSpec((tm, w), lambda i: (i, 0))
    tab = pl.BlockSpec((tm, LANES), lambda i: (i % n_tab_tiles, 0))
    in_specs = [row(d_model), tab, tab, tab,
                _const_spec((1, d_model)), _const_spec(wts["w_in"].shape),
                _const_spec((1, q_lora)), _const_spec(wts["w_uq"].shape),
                _const_spec((1, kv_lora)), _const_spec((1, HEAD_TILE)), _const_spec((1, HEAD_TILE)),
                _const_spec((1, HEAD_TILE)), _const_spec((1, HEAD_TILE)), _const_spec((1, vw)),
                _const_spec(wts["w_uk"].shape), _const_spec(wts["w_uv_p"].shape),
                _const_spec(wts["w_abs"].shape), _const_spec(wts["w_sel"].shape)]
    out_shape = [jax.ShapeDtypeStruct((t, d_conv), F32), jax.ShapeDtypeStruct((t, kv_lora), F32),
                 jax.ShapeDtypeStruct((t, QK_ROPE), F32), jax.ShapeDtypeStruct((t, 2 * d_model), BF16)]
    out_specs = [row(d_conv), row(kv_lora), row(QK_ROPE), row(2 * d_model)]
    if decode:
        widths = [N_HEADS * kv_lora, N_HEADS * QK_ROPE]
    else:
        widths = [qw, qw, vw]
    out_shape += [jax.ShapeDtypeStruct((t, w), BF16) for w in widths]
    out_specs += [row(w) for w in widths]
    kern = functools.partial(_in_proj_kernel, d_conv=d_conv, q_lora=q_lora, kv_lora=kv_lora,
                             d_model=d_model, decode=decode, sub_rows=sub_rows)
    return pl.pallas_call(
        kern, grid=(t // tm,), in_specs=in_specs, out_specs=out_specs, out_shape=out_shape,
        compiler_params=pltpu.CompilerParams(dimension_semantics=("arbitrary",),
                                             vmem_limit_bytes=VMEM_LIMIT_BYTES),
        name="in_proj_decode" if decode else "in_proj_prompt",
    )(x, *tabs, wts["g_mix"], wts["w_in"], wts["g_qa"], wts["w_uq"], wts["g_kva"], wts["g_q"], wts["g_k"],
      wts["q_one"], wts["k_bias"], wts["v_one"], wts["w_uk"], wts["w_uv_p"], wts["w_abs"], wts["w_sel"])


def _ln_silu(y, g, b):
    mu = jnp.mean(y, axis=-1, keepdims=True)
    yc = y - mu
    yn = yc * lax.rsqrt(jnp.mean(yc * yc, axis=-1, keepdims=True) + EPS) * g + b
    return yn * jax.nn.sigmoid(yn)


HALO = 32


SUBLANES = 8


def _conv_prompt_kernel(cur_ref, prev_ref, w_ref, b_ref, lng_ref, lnb_ref, y_ref, ext_ref, sh_ref, *, rows):
    ts = cur_ref.shape[1]
    width = w_ref.shape[0]
    ext_ref[HALO:, :] = cur_ref[0]
    prev = prev_ref[0]
    ext_ref[0:HALO, :] = jnp.where(pl.program_id(1) > 0, prev, jnp.zeros_like(prev))
    span = sh_ref.shape[1]
    for r in range(1, SUBLANES):
        sh_ref[r - 1] = ext_ref[r:r + span, :]
    base = HALO - (width - 1)
    for r0 in range(0, ts, rows):
        acc = None
        for k in range(width):
            kk = base + k
            start = r0 + (kk // SUBLANES) * SUBLANES
            if kk % SUBLANES == 0:
                window = ext_ref[start:start + rows, :]
            else:
                window = sh_ref[kk % SUBLANES - 1, start:start + rows, :]
            term = window * w_ref[k:k + 1, :]
            acc = term if acc is None else acc + term
        y = _ln_silu(acc + b_ref[...], lng_ref[...], lnb_ref[...])
        y_ref[r0:r0 + rows, :] = y.astype(y_ref.dtype)


def _conv_prompt(u, conv_w, conv_b, ln_g, ln_b, *, batch, ts, rows=64):
    t, d_conv = u.shape
    seq = t // batch
    width = conv_w.shape[0]
    assert width - 1 <= HALO and ts % HALO == 0 and seq % ts == 0 and HALO % SUBLANES == 0
    u3 = u.reshape(batch, seq, d_conv)
    nst = seq // ts
    span = ts + HALO - SUBLANES
    kern = functools.partial(_conv_prompt_kernel, rows=rows)
    return pl.pallas_call(
        kern, grid=(batch, nst),
        in_specs=[pl.BlockSpec((1, ts, d_conv), lambda b, i: (b, i, 0)),
                  pl.BlockSpec((1, HALO, d_conv), lambda b, i: (b, jnp.maximum(i * (ts // HALO) - 1, 0), 0)),
                  _const_spec(conv_w.shape), _const_spec((1, d_conv)), _const_spec((1, d_conv)),
                  _const_spec((1, d_conv))],
        out_specs=pl.BlockSpec((ts, d_conv), lambda b, i: (b * nst + i, 0)),
        out_shape=jax.ShapeDtypeStruct((t, d_conv), BF16),
        scratch_shapes=[pltpu.VMEM((ts + HALO, d_conv), F32), pltpu.VMEM((SUBLANES - 1, span, d_conv), F32)],
        compiler_params=pltpu.CompilerParams(dimension_semantics=("arbitrary", "arbitrary"),
                                             vmem_limit_bytes=VMEM_LIMIT_BYTES),
        name="conv_prompt",
    )(u3, u3, conv_w, conv_b, ln_g, ln_b)


def _conv_decode_kernel(state_ref, unew_ref, w_ref, b_ref, lng_ref, lnb_ref, y_ref):
    n_state = state_ref.shape[0]
    for t in range(unew_ref.shape[0]):
        acc = None
        for k in range(w_ref.shape[0]):
            j = t + k
            src = state_ref[j] if j < n_state else unew_ref[j - n_state]
            term = src * w_ref[k:k + 1, :]
            acc = term if acc is None else acc + term
        y_ref[t] = _ln_silu(acc + b_ref[...], lng_ref[...], lnb_ref[...]).astype(y_ref.dtype)


def _conv_decode(state_t, u_new_t, conv_w, conv_b, ln_g, ln_b):
    t_new, nb, d_conv = u_new_t.shape
    assert state_t.shape[0] == conv_w.shape[0] - 1
    full = lambda a: _const_spec(a.shape)
    return pl.pallas_call(
        _conv_decode_kernel, grid=(1,),
        in_specs=[full(state_t), full(u_new_t), full(conv_w), _const_spec((1, d_conv)),
                  _const_spec((1, d_conv)), _const_spec((1, d_conv))],
        out_specs=_const_spec((t_new, nb, d_conv)),
        out_shape=jax.ShapeDtypeStruct((t_new, nb, d_conv), BF16),
        compiler_params=pltpu.CompilerParams(dimension_semantics=("arbitrary",),
                                             vmem_limit_bytes=VMEM_LIMIT_BYTES),
        name="conv_decode",
    )(state_t, u_new_t, conv_w, conv_b, ln_g, ln_b)


def _flash_kernel(qi_tab, ki_tab, q_ref, k_ref, v_ref, o_ref, acc_ref, *m_refs, online):
    step = pl.program_id(2)
    qi = qi_tab[step]
    ki = ki_tab[step]
    tq = q_ref.shape[1]
    tk = k_ref.shape[1]
    n_heads = acc_ref.shape[0]

    @pl.when(ki == 0)
    def _():
        acc_ref[...] = jnp.zeros_like(acc_ref)
        if online:
            m_refs[0][...] = jnp.full_like(m_refs[0], NEG_BIG)

    def accumulate(masked):
        if masked:
            visible = (lax.broadcasted_iota(jnp.int32, (tq, tk), 1)
                       <= lax.broadcasted_iota(jnp.int32, (tq, tk), 0))
        for hh in range(n_heads):
            sl = slice(hh * HEAD_TILE, (hh + 1) * HEAD_TILE)
            s = _dot_nt(q_ref[0, :, sl], k_ref[0, :, sl])
            if masked:
                s = jnp.where(visible, s, NEG_BIG)
            if online:
                m_old = m_refs[0][hh]
                m_new = jnp.maximum(m_old, jnp.max(s, axis=-1, keepdims=True))
                m_refs[0][hh] = m_new
                p = jnp.exp2(s - m_new).astype(BF16)
                acc_ref[hh] = jnp.exp2(m_old - m_new) * acc_ref[hh] + _dot(p, v_ref[0, :, sl])
            else:
                acc_ref[hh] += _dot(jnp.exp2(s).astype(BF16), v_ref[0, :, sl])

    @pl.when(ki < qi)
    def _():
        accumulate(False)

    @pl.when(ki == qi)
    def _():
        accumulate(True)
        lane = lax.broadcasted_iota(jnp.int32, (tq, HEAD_TILE), 1)
        for pair in range(n_heads // 2):
            a0, a1 = acc_ref[2 * pair], acc_ref[2 * pair + 1]
            h0 = a0 / a0[:, ONES_LANE:ONES_LANE + 1]
            h1 = a1 / a1[:, ONES_LANE:ONES_LANE + 1]
            o_ref[0, :, pair * HEAD_TILE:(pair + 1) * HEAD_TILE] = jnp.where(
                lane < V_HEAD, h0, pltpu.roll(h1, V_HEAD, 1)).astype(o_ref.dtype)


def _flash_attention(q, k, v, *, batch, tq, heads_per_step, online):
    t = q.shape[0]
    seq = t // batch
    nq = seq // tq
    assert 2 * V_HEAD == HEAD_TILE and heads_per_step % 2 == 0 and N_HEADS % heads_per_step == 0
    q3, k3, v3 = (a.reshape(batch, seq, a.shape[1]) for a in (q, k, v))
    qi_list = [qi for qi in range(nq) for _ in range(qi + 1)]
    ki_list = [ki for qi in range(nq) for ki in range(qi + 1)]
    qi_tab = jnp.asarray(qi_list, jnp.int32)
    ki_tab = jnp.asarray(ki_list, jnp.int32)
    pair_tile = lambda tab: pl.BlockSpec((1, tq, heads_per_step * HEAD_TILE),
                                         lambda b, hp, s, qt, kt: (b, (qt if tab == "q" else kt)[s], hp))
    scratch = [pltpu.VMEM((heads_per_step, tq, HEAD_TILE), F32)]
    if online:
        scratch.append(pltpu.VMEM((heads_per_step, tq, 1), F32))
    grid_spec = pltpu.PrefetchScalarGridSpec(
        num_scalar_prefetch=2, grid=(batch, N_HEADS // heads_per_step, len(qi_list)),
        in_specs=[pair_tile("q"), pair_tile("k"), pair_tile("k")],
        out_specs=pl.BlockSpec((1, tq, heads_per_step * V_HEAD), lambda b, hp, s, qt, kt: (b, qt[s], hp)),
        scratch_shapes=scratch)
    out = pl.pallas_call(
        functools.partial(_flash_kernel, online=online), grid_spec=grid_spec,
        out_shape=jax.ShapeDtypeStruct((batch, seq, N_HEADS * V_HEAD), BF16),
        compiler_params=pltpu.CompilerParams(dimension_semantics=("arbitrary",) * 3,
                                             vmem_limit_bytes=VMEM_LIMIT_BYTES),
        name="flash_prompt_online" if online else "flash_prompt",
    )(qi_tab, ki_tab, q3, k3, v3)
    return out.reshape(t, N_HEADS * V_HEAD)


FETCH_SLOTS = 3
CHUNK_UNROLL = 1


def _paged_kernel(pt_ref, qa_ref, qpe_ref, latn_ref, kpen_ref, wukt_ref, lat_hbm, kpet_hbm,
                  o_ref, lat_buf, kpet_buf, sems, cb_ref, cbt_ref, cbn_ref, kpn_ref, s_ref, lhs_ref, *,
                  layer, n_pages, tk):
    b = pl.program_id(0)
    nb = pl.num_programs(0)
    past = n_pages * PAGE_SIZE
    n_chunks = past // tk
    t_new = latn_ref.shape[1]
    rows = qa_ref.shape[1]

    def page_copies(seq, slot, j):
        page = pt_ref[seq * n_pages + j]
        dst = pl.ds(j * PAGE_SIZE, PAGE_SIZE)
        return (pltpu.make_async_copy(lat_hbm.at[layer, page], lat_buf.at[slot, dst], sems.at[slot, 0]),
                pltpu.make_async_copy(kpet_hbm.at[layer, page], kpet_buf.at[slot, j], sems.at[slot, 1]))

    def start_fetch(seq, slot, pages=None):
        for j in (range(n_pages) if pages is None else pages):
            lat_cp, kpe_cp = page_copies(seq, slot, j)
            lat_cp.start()
            kpe_cp.start(priority=1)

    def wait_fetch(seq, slot):
        for j in range(n_pages):
            for cp in page_copies(seq, slot, j):
                cp.wait()

    slot = b % FETCH_SLOTS

    @pl.when(b == 0)
    def _():
        for s in range(FETCH_SLOTS - 1):
            start_fetch(jnp.minimum(s, nb - 1), s)
        lhs_ref[0:N_HEADS * QK_NOPE, :] = wukt_ref[...]
        cbn_ref[...] = jnp.zeros_like(cbn_ref)
        kpn_ref[...] = jnp.zeros_like(kpn_ref)

    lhs_ref[N_HEADS * QK_NOPE:, :] = qa_ref[0]
    qpe = qpe_ref[0]
    ones = jnp.ones((N_HEADS, QK_ROPE), BF16)
    pages_per_chunk = tk // PAGE_SIZE

    def scores(cbt, ss_pe, s_pe):
        r = _dot(lhs_ref[...], cbt)
        n = cbt.shape[1]
        kn = r[:N_HEADS * QK_NOPE]
        ss = jnp.sum((kn * kn).reshape(QK_NOPE, N_HEADS, n), axis=0) + ss_pe
        inv = lax.rsqrt(ss * (1.0 / QK_HEAD) + EPS)
        return (r[N_HEADS * QK_NOPE:] + s_pe) * jnp.concatenate([inv] * t_new, axis=0)

    wait_fetch(b, slot)

    def lane_tiles(a):
        return [a[:, i:i + LANES] for i in range(0, a.shape[1], LANES)]

    def chunk_group(g, m):
        for j in range(CHUNK_UNROLL):
            c = g * CHUNK_UNROLL + j
            st = pl.multiple_of(c * tk, tk)
            cb = lat_buf[slot, pl.ds(st, tk), :].astype(BF16)
            cb_ref[pl.ds(st, tk), :] = cb
            kpt = jnp.concatenate([kpet_buf[slot, c * pages_per_chunk + i] for i in range(pages_per_chunk)],
                                  axis=1)
            cbt_ref[j] = cb.T
            sc = scores(cbt_ref[j], jnp.sum(kpt * kpt, axis=0, keepdims=True), _dot(qpe, kpt.astype(BF16)))
            s_ref[c] = sc
            m = functools.reduce(jnp.maximum, [m] + lane_tiles(sc))
        return m

    m = lax.fori_loop(0, n_chunks // CHUNK_UNROLL, chunk_group, jnp.full((rows, LANES), NEG_BIG, F32))
    m = jnp.max(m, axis=-1, keepdims=True)

    cbn_ref[0:t_new, :] = latn_ref[0].astype(BF16)
    kpn_ref[0:t_new, :] = kpen_ref[0]
    cbn = cbn_ref[...]
    kpn = kpn_ref[...]
    sq = kpn * kpn
    hi = sq.astype(BF16)
    lo = (sq - hi.astype(F32)).astype(BF16)
    col = lax.broadcasted_iota(jnp.int32, (rows, LANES), 1)
    qrow = lax.broadcasted_iota(jnp.int32, (rows, LANES), 0) // N_HEADS
    s_new = scores(cbn.astype(F32).T.astype(BF16), _dot_nt(ones, hi) + _dot_nt(ones, lo),
                   _dot_nt(qpe, kpn.astype(BF16)))
    s_new = jnp.where(col <= qrow, s_new, NEG_BIG)
    m_new = jnp.max(s_new, axis=-1, keepdims=True)
    p_new = jnp.exp2(s_new - m_new)
    l_new = jnp.sum(p_new, axis=-1, keepdims=True)
    o_new = _dot(p_new.astype(BF16), cbn)

    ahead = FETCH_SLOTS - 1
    nxt = jnp.minimum(b + ahead, nb - 1)
    nxt_slot = (b + ahead) % FETCH_SLOTS
    pages_per_pv = n_pages // n_chunks
    l = jnp.zeros((rows, LANES), F32)
    o = jnp.zeros((rows, cb_ref.shape[1]), F32)
    for c in range(n_chunks):
        start_fetch(nxt, nxt_slot, range(c * pages_per_pv, (c + 1) * pages_per_pv))
        p = jnp.exp2(s_ref[c] - m)
        l = functools.reduce(jnp.add, [l] + lane_tiles(p))
        o = o + _dot(p.astype(BF16), cb_ref[c * tk:(c + 1) * tk, :])
    l = jnp.sum(l, axis=-1, keepdims=True)

    m_all = jnp.maximum(m, m_new)
    w_past = jnp.exp2(m - m_all)
    w_new = jnp.exp2(m_new - m_all)
    o_ref[0] = ((o * w_past + o_new * w_new) / (l * w_past + l_new * w_new)).astype(o_ref.dtype)

    @pl.when(b == nb - 1)
    def _():
        for d in range(1, FETCH_SLOTS):
            wait_fetch(nxt, (b + d) % FETCH_SLOTS)


def _paged_attention(page_table, qa, qpe, lat_new, kpe_new, cache_lat, cache_kpe_t, wukt, *, layer, tk):
    nb, n_pages = page_table.shape
    assert tk % PAGE_SIZE == 0 and cache_kpe_t.shape[2:] == (QK_ROPE, PAGE_SIZE)
    assert (n_pages * PAGE_SIZE) % (tk * CHUNK_UNROLL) == 0
    t_new = qa.shape[0] // nb
    kv_lora = cache_lat.shape[-1]
    past = n_pages * PAGE_SIZE
    rows = t_new * N_HEADS
    qa3 = qa.reshape(nb, rows, kv_lora)
    qpe3 = qpe.reshape(nb, rows, QK_ROPE)
    ln3 = lat_new.reshape(nb, t_new, kv_lora)
    kn3 = kpe_new.reshape(nb, t_new, QK_ROPE)
    per_seq = lambda r, w: pl.BlockSpec((1, r, w), lambda b, pt: (b, 0, 0))
    const = lambda a: pl.BlockSpec(a.shape, lambda b, pt: (0,) * a.ndim)
    grid_spec = pltpu.PrefetchScalarGridSpec(
        num_scalar_prefetch=1, grid=(nb,),
        in_specs=[per_seq(rows, kv_lora), per_seq(rows, QK_ROPE), per_seq(t_new, kv_lora),
                  per_seq(t_new, QK_ROPE), const(wukt), pl.BlockSpec(memory_space=pl.ANY),
                  pl.BlockSpec(memory_space=pl.ANY)],
        out_specs=per_seq(rows, kv_lora),
        scratch_shapes=[pltpu.VMEM((FETCH_SLOTS, past, kv_lora), F32),
                        pltpu.VMEM((FETCH_SLOTS, n_pages, QK_ROPE, PAGE_SIZE), F32),
                        pltpu.SemaphoreType.DMA((FETCH_SLOTS, 2)),
                        pltpu.VMEM((past, kv_lora), BF16), pltpu.VMEM((CHUNK_UNROLL, kv_lora, tk), BF16),
                        pltpu.VMEM((LANES, kv_lora), BF16),
                        pltpu.VMEM((LANES, QK_ROPE), F32), pltpu.VMEM((past // tk, rows, tk), F32),
                        pltpu.VMEM((N_HEADS * QK_NOPE + rows, kv_lora), BF16)])
    kern = functools.partial(_paged_kernel, layer=layer, n_pages=n_pages, tk=tk)
    out = pl.pallas_call(
        kern, grid_spec=grid_spec,
        out_shape=jax.ShapeDtypeStruct((nb, rows, kv_lora), BF16),
        compiler_params=pltpu.CompilerParams(dimension_semantics=("arbitrary",),
                                             vmem_limit_bytes=VMEM_LIMIT_BYTES),
        name="paged_decode",
    )(page_table.reshape(-1), qa3, qpe3, ln3, kn3, wukt, cache_lat, cache_kpe_t)
    return out.reshape(nb * t_new, N_HEADS * kv_lora)


FF_CHUNK = 256


def _out_ffn_kernel(x_ref, attn_ref, y_ref, gates_ref, wco_ref, womla_ref, wout_ref, gffn_ref,
                    wg_ref, wu_ref, wd_ref, *rest, latent_attn):
    if latent_attn:
        wuv_ref, o_ref, act_ref = rest
        attn = _dot(attn_ref[...], wuv_ref[...]).astype(BF16)
    else:
        o_ref, act_ref = rest
        attn = attn_ref[...]
    y_cur = y_ref[...]
    d_model = x_ref.shape[1]
    d_ff = wg_ref.shape[1]
    half = d_model // 2
    for c0 in (0, half):
        sl = slice(c0, c0 + half)
        merged = (gates_ref[:, sl] * _dot(y_cur, wco_ref[:, sl])
                  + gates_ref[:, d_model + c0:d_model + c0 + half] * _dot(attn, womla_ref[:, sl]))
        act_ref[:, sl] = merged.astype(BF16)
    o_ref[...] = x_ref[...] + _dot(act_ref[:, :d_model], wout_ref[...])
    x1 = o_ref[...]
    h2 = (x1 * _rms_scale(x1) * gffn_ref[...]).astype(BF16)
    for c0 in range(0, d_ff, FF_CHUNK):
        sl = slice(c0, c0 + FF_CHUNK)
        g = _dot(h2, wg_ref[:, sl])
        act_ref[:, sl] = (g * jax.nn.sigmoid(g) * _dot(h2, wu_ref[:, sl])).astype(BF16)
    o_ref[...] += _dot(act_ref[...], wd_ref[...])


def _out_ffn(x, attn, y_act, gates, wts, *, tm, latent_attn):
    t, d_model = x.shape
    d_ff = wts["w_gate"].shape[1]
    assert d_ff % FF_CHUNK == 0 and d_ff >= d_model
    row = lambda w: pl.BlockSpec((tm, w), lambda i: (i, 0))
    wspec = lambda a: _const_spec(a.shape, single_buffer=True)
    weights = [wts["w_conv_out"], wts["w_o_mla"], wts["w_out"], wts["g_ffn"], wts["w_gate"], wts["w_up"],
               wts["w_down"]] + ([wts["w_uv_bd"]] if latent_attn else [])
    return pl.pallas_call(
        functools.partial(_out_ffn_kernel, latent_attn=latent_attn), grid=(t // tm,),
        in_specs=[row(d_model), row(attn.shape[1]), row(y_act.shape[1]), row(2 * d_model)]
                 + [wspec(w) for w in weights],
        out_specs=row(d_model),
        out_shape=jax.ShapeDtypeStruct((t, d_model), F32),
        scratch_shapes=[pltpu.VMEM((tm, d_ff), BF16)],
        compiler_params=pltpu.CompilerParams(dimension_semantics=("arbitrary",),
                                             vmem_limit_bytes=VMEM_LIMIT_BYTES),
        name="out_ffn_decode" if latent_attn else "out_ffn_prompt",
    )(x, attn, y_act, gates, *weights)


def _head_tile_cols(w_rope, w_nope):
    lead = w_nope.shape[:-2]
    pad = jnp.zeros(lead + (N_HEADS, HEAD_TILE - QK_HEAD), w_nope.dtype)
    tile = jnp.concatenate([w_rope, w_nope, pad], axis=-1)
    return tile.reshape(lead + (N_HEADS * HEAD_TILE,))


def _gain_tile(g_compact):
    g = g_compact.astype(F32)
    return jnp.concatenate([g[QK_NOPE:], g[QK_NOPE:], g[:QK_NOPE], jnp.zeros((HEAD_TILE - QK_HEAD,), F32)])[None]


def _rope_tables(pos):
    inv_freq = 1.0 / (ROPE_THETA ** (jnp.arange(0, QK_ROPE, 2, dtype=F32) / QK_ROPE))
    ang = pos.astype(F32)[:, None] * inv_freq[None, :]
    c, s = jnp.cos(ang), jnp.sin(ang)
    n = pos.shape[0]
    z16 = jnp.zeros((n, ROPE_HALF), F32)
    rest = LANES - QK_ROPE
    c_tab = jnp.concatenate([c, c, jnp.ones((n, QK_NOPE), F32), jnp.zeros((n, rest - QK_NOPE), F32)], axis=1)
    sl_tab = jnp.concatenate([-s, z16, jnp.zeros((n, rest), F32)], axis=1)
    sr_tab = jnp.concatenate([z16, s, jnp.zeros((n, rest), F32)], axis=1)
    return c_tab, sl_tab, sr_tab


def _layer_weights(l, norm_mix_g, w_in, q_a_norm_g, w_uq, kv_a_norm_g, w_uk, w_uv, q_norm_g, k_norm_g,
                   w_o_mla, w_conv_out, w_out, norm_ffn_g, w_gate, w_up, w_down):
    d_model = w_in.shape[1]
    kv_lora = w_uk.shape[1]
    q_lora = w_uq.shape[1]
    d_conv = w_conv_out.shape[1]
    o1 = 2 * d_conv
    o2 = o1 + q_lora
    o3 = o2 + kv_lora
    o4 = o3 + QK_ROPE
    wi = w_in[l]
    w_in_p = jnp.concatenate([wi[:, :o3], wi[:, o3:o4], jnp.zeros((d_model, HEAD_TILE - QK_ROPE), wi.dtype),
                              wi[:, o4:]], axis=1).astype(BF16)
    uq = w_uq[l].reshape(q_lora, N_HEADS, QK_HEAD)
    w_uq_p = _head_tile_cols(uq[..., QK_NOPE:], uq[..., :QK_NOPE]).astype(BF16)
    uk = w_uk[l]
    w_uk_p = _head_tile_cols(jnp.zeros((kv_lora, N_HEADS, QK_ROPE), uk.dtype), uk).astype(BF16)
    w_ukt = jnp.transpose(uk, (2, 1, 0)).reshape(QK_NOPE * N_HEADS, kv_lora).astype(BF16)
    uv = w_uv[l]
    w_uv_p = jnp.concatenate([uv, jnp.zeros((kv_lora, N_HEADS, HEAD_TILE - V_HEAD), uv.dtype)],
                             axis=-1).reshape(kv_lora, N_HEADS * HEAD_TILE).astype(BF16)
    lane = jnp.arange(HEAD_TILE)
    bound = (1.02 * LOG2E * math.sqrt(QK_HEAD)) * jnp.max(jnp.abs(q_norm_g[l])) * jnp.max(jnp.abs(k_norm_g[l]))
    q_one = jnp.where(lane == BIAS_LANE, 1.0, 0.0).astype(F32)[None]
    k_bias = jnp.where(lane == BIAS_LANE, -bound, 0.0).astype(F32)[None]
    v_one = jnp.tile(jnp.where(lane == ONES_LANE, 1.0, 0.0).astype(F32), N_HEADS)[None]
    ukt = jnp.transpose(uk, (1, 2, 0))
    w_abs = jnp.concatenate([jnp.zeros((N_HEADS, QK_ROPE, kv_lora), uk.dtype), ukt,
                             jnp.zeros((N_HEADS, HEAD_TILE - QK_HEAD, kv_lora), uk.dtype)], axis=1).astype(BF16)
    lane_ids = jnp.arange(HEAD_TILE)[None, :, None]
    col_ids = jnp.arange(N_HEADS * QK_ROPE)[None, None, :]
    head_ids = jnp.arange(N_HEADS)[:, None, None]
    w_sel = ((lane_ids < QK_ROPE) & (col_ids == head_ids * QK_ROPE + lane_ids)).astype(BF16)
    eye_h = jnp.eye(N_HEADS, dtype=uv.dtype)
    w_uv_bd = jnp.einsum("chd,hg->hcgd", uv, eye_h).reshape(N_HEADS * kv_lora, N_HEADS * V_HEAD).astype(BF16)
    g_k = _gain_tile(k_norm_g[l])
    return dict(
        g_mix=norm_mix_g[l][None].astype(F32), w_in=w_in_p, g_qa=q_a_norm_g[l][None].astype(F32), w_uq=w_uq_p,
        g_kva=kv_a_norm_g[l][None].astype(F32), g_q=_gain_tile(q_norm_g[l]), g_k=g_k,
        w_uk=w_uk_p, w_uv_p=w_uv_p, w_uv_bd=w_uv_bd, q_one=q_one, k_bias=k_bias, v_one=v_one, score_bound=bound,
        w_ukt=w_ukt, w_abs=w_abs, w_sel=w_sel, w_conv_out=w_conv_out[l].astype(BF16),
        w_o_mla=w_o_mla[l].astype(BF16),
        w_out=w_out[l].astype(BF16), g_ffn=norm_ffn_g[l][None].astype(F32), w_gate=w_gate[l].astype(BF16),
        w_up=w_up[l].astype(BF16), w_down=w_down[l].astype(BF16))


def kernel(x_prompt, x_sample, cache_kv_latent, cache_k_rope, state_conv, page_table, norm_mix_g, w_in,
           q_a_norm_g, w_uq, kv_a_norm_g, w_uk, w_uv, q_norm_g, k_norm_g, w_o_mla, conv_w, conv_b, conv_ln_g,
           conv_ln_b, w_conv_out, w_out, norm_ffn_g, w_gate, w_up, w_down):
    batch, seq, d_model = x_prompt.shape
    nb, t_new, _ = x_sample.shape
    depth = w_in.shape[0]
    n_pages = page_table.shape[1]
    past = n_pages * PAGE_SIZE
    n_state = state_conv.shape[2]
    d_conv = state_conv.shape[3]
    kv_lora = cache_kv_latent.shape[-1]

    tabs_p = _rope_tables(jnp.arange(seq))
    tabs_s = _rope_tables(jnp.tile(past + jnp.arange(t_new), nb))

    yp = x_prompt.reshape(batch * seq, d_model)
    ys = x_sample.reshape(nb * t_new, d_model)
    outs = [[] for _ in range(6)]
    for l in range(depth):
        wts = _layer_weights(l, norm_mix_g, w_in, q_a_norm_g, w_uq, kv_a_norm_g, w_uk, w_uv, q_norm_g,
                             k_norm_g, w_o_mla, w_conv_out, w_out, norm_ffn_g, w_gate, w_up, w_down)
        cb, lg, lb = (a[l][None].astype(F32) for a in (conv_b, conv_ln_g, conv_ln_b))

        u, lat, kpe, gates, q, k, v = _in_proj(yp, tabs_p, wts, tm=512, sub_rows=256, decode=False)
        attn = lax.cond(wts["score_bound"] <= FIXED_SHIFT_MAX_BOUND,
                        functools.partial(_flash_attention, batch=batch, tq=512, heads_per_step=N_HEADS,
                                          online=False),
                        functools.partial(_flash_attention, batch=batch, tq=512, heads_per_step=2,
                                          online=True), q, k, v)
        y_act = _conv_prompt(u, conv_w[l].astype(F32), cb, lg, lb, batch=batch, ts=512)
        yp = _out_ffn(yp, attn, y_act, gates, wts, tm=512, latent_attn=False)
        outs[0].append(lat.reshape(batch, seq, kv_lora))
        outs[1].append(kpe.reshape(batch, seq, QK_ROPE))
        outs[2].append(u.reshape(batch, seq, d_conv)[:, seq - n_state:])

        u, lat, kpe, gates, qa, qpe = _in_proj(ys, tabs_s, wts, tm=nb * t_new, sub_rows=nb * t_new // 2, decode=True)
        u_t = jnp.transpose(u.reshape(nb, t_new, d_conv), (1, 0, 2))
        state_t = jnp.transpose(state_conv[l], (1, 0, 2))
        y_act = _conv_decode(state_t, u_t, conv_w[l].astype(F32), cb, lg, lb)
        y_act = jnp.transpose(y_act, (1, 0, 2)).reshape(nb * t_new, d_conv)
        attn = _paged_attention(page_table, qa, qpe, lat, kpe, cache_kv_latent, jnp.swapaxes(cache_k_rope, 2, 3),
                                wts["w_ukt"], layer=l, tk=4096)
        ys = _out_ffn(ys, attn, y_act, gates, wts, tm=nb * t_new, latent_attn=True)
        outs[3].append(lat.reshape(nb, t_new, kv_lora))
        outs[4].append(kpe.reshape(nb, t_new, QK_ROPE))
        outs[5].append(jnp.transpose(jnp.concatenate([state_t[t_new:], u_t], axis=0), (1, 0, 2)))

    return (yp.reshape(batch, seq, d_model), ys.reshape(nb, t_new, d_model),
            *(jnp.stack(o) for o in outs))
```

```python
import functools
import math

import jax
import jax.numpy as jnp
from jax import lax
from jax.experimental import pallas as pl
from jax.experimental.pallas import tpu as pltpu

N_HEADS = 8
QK_NOPE = 64
QK_ROPE = 32
ROPE_HALF = QK_ROPE // 2
QK_HEAD = QK_NOPE + QK_ROPE
V_HEAD = 64
ROPE_THETA = 10000.0
EPS = 1e-6
PAGE_SIZE = 128

LANES = 128
HEAD_TILE = LANES
NOPE_OFF = QK_ROPE
VMEM_LIMIT_BYTES = 56 * 1024 * 1024

BIAS_LANE = QK_HEAD
ONES_LANE = V_HEAD
LOG2E = 1.4426950408889634
FIXED_SHIFT_MAX_BOUND = 40.0

F32 = jnp.float32
BF16 = jnp.bfloat16
NEG_BIG = -1e30


def _dot(a, b):
    return jnp.dot(a, b, preferred_element_type=F32)


def _dot_nt(a, b):
    return lax.dot_general(a, b, (((1,), (1,)), ((), ())), preferred_element_type=F32)


def _rms_scale(x):
    return lax.rsqrt(jnp.mean(x * x, axis=-1, keepdims=True) + EPS)


def _compiler_params(n_grid_axes):
    return pltpu.CompilerParams(dimension_semantics=("arbitrary",) * n_grid_axes,
                                vmem_limit_bytes=VMEM_LIMIT_BYTES)


def _const_spec(shape, single_buffer=False):
    nd = len(shape)
    if single_buffer:
        return pl.BlockSpec(shape, lambda *_: (0,) * nd, pipeline_mode=pl.Buffered(1))
    return pl.BlockSpec(shape, lambda *_: (0,) * nd)


def _rope_tile(t, c_tab, sl_tab, sr_tab):
    return t * c_tab + pltpu.roll(t, ROPE_HALF, 1) * sr_tab + pltpu.roll(t, LANES - ROPE_HALF, 1) * sl_tab


N_ROW_INPUTS = 4
N_WEIGHT_INPUTS = 14


def _in_proj_kernel(*refs, sub_rows, **kw):
    n_in = N_ROW_INPUTS + N_WEIGHT_INPUTS
    rows, weights, outs = refs[:N_ROW_INPUTS], refs[N_ROW_INPUTS:n_in], refs[n_in:]
    for r0 in range(0, rows[0].shape[0], sub_rows):
        view = lambda ref: ref.at[pl.ds(r0, sub_rows)]
        _in_proj_rows(*map(view, rows), *weights, *map(view, outs), **kw)


def _in_proj_rows(x_ref, ctab_ref, sltab_ref, srtab_ref, gmix_ref, w_in_ref, gqa_ref, w_uq_ref,
                  gkva_ref, gq_ref, gk_ref, qone_ref, kbias_ref, vone_ref, w_uk_ref, w_uv_ref,
                  wabs_ref, sel_ref, u_ref, lat_ref, kpe_ref, gates_ref, *attn_refs,
                  d_conv, q_lora, kv_lora, d_model, decode):
    x = x_ref[...]
    hb = (x * _rms_scale(x) * gmix_ref[...]).astype(BF16)
    c_tab, sl_tab, sr_tab = ctab_ref[...], sltab_ref[...], srtab_ref[...]

    o_cq = 2 * d_conv
    o_ckv = o_cq + q_lora
    o_kpe = o_ckv + kv_lora
    o_gate = o_kpe + HEAD_TILE

    ag = _dot(hb, w_in_ref[:, 0:o_cq])
    u_ref[...] = ag[:, :d_conv] * jax.nn.sigmoid(ag[:, d_conv:])

    cq = _dot(hb, w_in_ref[:, o_cq:o_ckv])
    cqn = (cq * _rms_scale(cq) * gqa_ref[...]).astype(BF16)
    q = _dot(cqn, w_uq_ref[...])
    q_scale = LOG2E / math.sqrt(QK_HEAD)
    qpe = None
    for h in range(N_HEADS):
        sl = slice(h * HEAD_TILE, (h + 1) * HEAD_TILE)
        qh = _rope_tile(q[:, sl], c_tab, sl_tab, sr_tab)
        ss = jnp.sum(qh * qh, axis=-1, keepdims=True)
        qh = qh * lax.rsqrt(ss * (1.0 / QK_HEAD) + EPS) * (gq_ref[...] * q_scale)
        if decode:
            qa_ref, qpe_ref = attn_refs
            qg = (qh * gk_ref[...]).astype(BF16)
            qa_ref[:, h * kv_lora:(h + 1) * kv_lora] = _dot(qg, wabs_ref[h]).astype(qa_ref.dtype)
            part = _dot(qg, sel_ref[h])
            qpe = part if qpe is None else qpe + part
        else:
            attn_refs[0][:, sl] = (qh + qone_ref[...]).astype(attn_refs[0].dtype)
    if decode:
        qpe_ref[...] = qpe.astype(qpe_ref.dtype)

    ck = _dot(hb, w_in_ref[:, o_ckv:o_gate])
    ckv = ck[:, :kv_lora]
    lat = ckv * _rms_scale(ckv) * gkva_ref[...]
    lat_ref[...] = lat
    kpe = _rope_tile(ck[:, kv_lora:], c_tab, sl_tab, sr_tab)
    kpe_ref[...] = kpe[:, :QK_ROPE]

    for g in range(2):
        sl = slice(g * d_model, (g + 1) * d_model)
        gate = jax.nn.sigmoid(_dot(hb, w_in_ref[:, o_gate + g * d_model:o_gate + (g + 1) * d_model]))
        gates_ref[:, sl] = gate.astype(gates_ref.dtype)

    if not decode:
        _, k_ref, v_ref = attn_refs
        latb = lat.astype(BF16)
        v_ref[...] = (_dot(latb, w_uv_ref[...]) + vone_ref[...]).astype(v_ref.dtype)
        kn = _dot(latb, w_uk_ref[...])
        for h in range(N_HEADS):
            sl = slice(h * HEAD_TILE, (h + 1) * HEAD_TILE)
            kh = kn[:, sl] + kpe
            ss = jnp.sum(kh * kh, axis=-1, keepdims=True)
            kh = kh * lax.rsqrt(ss * (1.0 / QK_HEAD) + EPS) * gk_ref[...] + kbias_ref[...]
            k_ref[:, sl] = kh.astype(k_ref.dtype)


def _in_proj(x, tabs, wts, *, tm, sub_rows, decode):
    t, d_model = x.shape
    n_tab_tiles = tabs[0].shape[0] // tm
    d_in_p = wts["w_in"].shape[1]
    q_lora = wts["w_uq"].shape[0]
    kv_lora = wts["w_uk"].shape[0]
    d_conv = (d_in_p - q_lora - kv_lora - HEAD_TILE - 2 * d_model) // 2
    qw = N_HEADS * HEAD_TILE
    vw = N_HEADS * HEAD_TILE

    row = lambda w: pl.BlockSpec((tm, w), lambda i: (i, 0))
    tab = pl.BlockSpec((tm, LANES), lambda i: (i % n_tab_tiles, 0))
    in_specs = [row(d_model), tab, tab, tab,
                _const_spec((1, d_model)), _const_spec(wts["w_in"].shape),
                _const_spec((1, q_lora)), _const_spec(wts["w_uq"].shape),
                _const_spec((1, kv_lora)), _const_spec((1, HEAD_TILE)), _const_spec((1, HEAD_TILE)),
                _const_spec((1, HEAD_TILE)), _const_spec((1, HEAD_TILE)), _const_spec((1, vw)),
                _const_spec(wts["w_uk"].shape), _const_spec(wts["w_uv_p"].shape),
                _const_spec(wts["w_abs"].shape), _const_spec(wts["w_sel"].shape)]
    out_shape = [jax.ShapeDtypeStruct((t, d_conv), F32), jax.ShapeDtypeStruct((t, kv_lora), F32),
                 jax.ShapeDtypeStruct((t, QK_ROPE), F32), jax.ShapeDtypeStruct((t, 2 * d_model), BF16)]
    out_specs = [row(d_conv), row(kv_lora), row(QK_ROPE), row(2 * d_model)]
    if decode:
        widths = [N_HEADS * kv_lora, N_HEADS * QK_ROPE]
    else:
        widths = [qw, qw, vw]
    out_shape += [jax.ShapeDtypeStruct((t, w), BF16) for w in widths]
    out_specs += [row(w) for w in widths]
    kern = functools.partial(_in_proj_kernel, d_conv=d_conv, q_lora=q_lora, kv_lora=kv_lora,
                             d_model=d_model, decode=decode, sub_rows=sub_rows)
    return pl.pallas_call(
        kern, grid=(t // tm,), in_specs=in_specs, out_specs=out_specs, out_shape=out_shape,
        compiler_params=_compiler_params(1),
        name="in_proj_decode" if decode else "in_proj_prompt",
    )(x, *tabs, wts["g_mix"], wts["w_in"], wts["g_qa"], wts["w_uq"], wts["g_kva"], wts["g_q"], wts["g_k"],
      wts["q_one"], wts["k_bias"], wts["v_one"], wts["w_uk"], wts["w_uv_p"], wts["w_abs"], wts["w_sel"])


def _ln_silu(y, g, b):
    mu = jnp.mean(y, axis=-1, keepdims=True)
    yc = y - mu
    yn = yc * lax.rsqrt(jnp.mean(yc * yc, axis=-1, keepdims=True) + EPS) * g + b
    return yn * jax.nn.sigmoid(yn)


HALO = 32


SUBLANES = 8


def _conv_prompt_kernel(cur_ref, prev_ref, w_ref, b_ref, lng_ref, lnb_ref, y_ref, ext_ref, sh_ref, *, rows):
    ts = cur_ref.shape[1]
    width = w_ref.shape[0]
    ext_ref[HALO:, :] = cur_ref[0]
    prev = prev_ref[0]
    ext_ref[0:HALO, :] = jnp.where(pl.program_id(1) > 0, prev, jnp.zeros_like(prev))
    span = sh_ref.shape[1]
    for r in range(1, SUBLANES):
        sh_ref[r - 1] = ext_ref[r:r + span, :]
    base = HALO - (width - 1)
    for r0 in range(0, ts, rows):
        acc = None
        for k in range(width):
            kk = base + k
            start = r0 + (kk // SUBLANES) * SUBLANES
            if kk % SUBLANES == 0:
                window = ext_ref[start:start + rows, :]
            else:
                window = sh_ref[kk % SUBLANES - 1, start:start + rows, :]
            term = window * jnp.tile(w_ref[k], (rows // SUBLANES, 1))
            acc = term if acc is None else acc + term
        y = _ln_silu(acc + b_ref[...], lng_ref[...], lnb_ref[...])
        y_ref[r0:r0 + rows, :] = y.astype(y_ref.dtype)


def _conv_prompt(u, conv_w, conv_b, ln_g, ln_b, *, batch, ts, rows=64):
    t, d_conv = u.shape
    seq = t // batch
    width = conv_w.shape[0]
    assert width - 1 <= HALO and ts % HALO == 0 and seq % ts == 0 and HALO % SUBLANES == 0
    u3 = u.reshape(batch, seq, d_conv)
    nst = seq // ts
    span = ts + HALO - SUBLANES
    conv_w = jnp.broadcast_to(conv_w[:, None, :], (width, SUBLANES, d_conv))
    kern = functools.partial(_conv_prompt_kernel, rows=rows)
    return pl.pallas_call(
        kern, grid=(batch, nst),
        in_specs=[pl.BlockSpec((1, ts, d_conv), lambda b, i: (b, i, 0)),
                  pl.BlockSpec((1, HALO, d_conv), lambda b, i: (b, jnp.maximum(i * (ts // HALO) - 1, 0), 0)),
                  _const_spec(conv_w.shape), _const_spec((1, d_conv)), _const_spec((1, d_conv)),
                  _const_spec((1, d_conv))],
        out_specs=pl.BlockSpec((ts, d_conv), lambda b, i: (b * nst + i, 0)),
        out_shape=jax.ShapeDtypeStruct((t, d_conv), BF16),
        scratch_shapes=[pltpu.VMEM((ts + HALO, d_conv), F32), pltpu.VMEM((SUBLANES - 1, span, d_conv), F32)],
        compiler_params=_compiler_params(2),
        name="conv_prompt",
    )(u3, u3, conv_w, conv_b, ln_g, ln_b)


def _conv_decode_kernel(state_ref, unew_ref, w_ref, b_ref, lng_ref, lnb_ref, y_ref):
    n_state = state_ref.shape[0]
    for t in range(unew_ref.shape[0]):
        acc = None
        for k in range(w_ref.shape[0]):
            j = t + k
            src = state_ref[j] if j < n_state else unew_ref[j - n_state]
            term = src * w_ref[k:k + 1, :]
            acc = term if acc is None else acc + term
        y_ref[t] = _ln_silu(acc + b_ref[...], lng_ref[...], lnb_ref[...]).astype(y_ref.dtype)


def _conv_decode(state_t, u_new_t, conv_w, conv_b, ln_g, ln_b):
    t_new, nb, d_conv = u_new_t.shape
    assert state_t.shape[0] == conv_w.shape[0] - 1
    full = lambda a: _const_spec(a.shape)
    return pl.pallas_call(
        _conv_decode_kernel, grid=(1,),
        in_specs=[full(state_t), full(u_new_t), full(conv_w), _const_spec((1, d_conv)),
                  _const_spec((1, d_conv)), _const_spec((1, d_conv))],
        out_specs=_const_spec((t_new, nb, d_conv)),
        out_shape=jax.ShapeDtypeStruct((t_new, nb, d_conv), BF16),
        compiler_params=_compiler_params(1),
        name="conv_decode",
    )(state_t, u_new_t, conv_w, conv_b, ln_g, ln_b)


def _flash_kernel(qi_tab, ki_tab, q_ref, k_ref, v_ref, o_ref, acc_ref, *m_refs, online):
    step = pl.program_id(2)
    qi = qi_tab[step]
    ki = ki_tab[step]
    tq = q_ref.shape[1]
    tk = k_ref.shape[1]
    n_heads = acc_ref.shape[0]

    @pl.when(ki == 0)
    def _():
        acc_ref[...] = jnp.zeros_like(acc_ref)
        if online:
            m_refs[0][...] = jnp.full_like(m_refs[0], NEG_BIG)

    def accumulate(masked):
        if masked:
            visible = (lax.broadcasted_iota(jnp.int32, (tq, tk), 1)
                       <= lax.broadcasted_iota(jnp.int32, (tq, tk), 0))
        for hh in range(n_heads):
            sl = slice(hh * HEAD_TILE, (hh + 1) * HEAD_TILE)
            s = _dot_nt(q_ref[0, :, sl], k_ref[0, :, sl])
            if masked:
                s = jnp.where(visible, s, NEG_BIG)
            if online:
                m_old = m_refs[0][hh]
                m_new = jnp.maximum(m_old, jnp.max(s, axis=-1, keepdims=True))
                m_refs[0][hh] = m_new
                p = jnp.exp2(s - m_new).astype(BF16)
                acc_ref[hh] = jnp.exp2(m_old - m_new) * acc_ref[hh] + _dot(p, v_ref[0, :, sl])
            else:
                acc_ref[hh] += _dot(jnp.exp2(s).astype(BF16), v_ref[0, :, sl])

    @pl.when(ki < qi)
    def _():
        accumulate(False)

    @pl.when(ki == qi)
    def _():
        accumulate(True)
        lane = lax.broadcasted_iota(jnp.int32, (tq, HEAD_TILE), 1)
        for pair in range(n_heads // 2):
            a0, a1 = acc_ref[2 * pair], acc_ref[2 * pair + 1]
            h0 = a0 / a0[:, ONES_LANE:ONES_LANE + 1]
            h1 = a1 / a1[:, ONES_LANE:ONES_LANE + 1]
            o_ref[0, :, pair * HEAD_TILE:(pair + 1) * HEAD_TILE] = jnp.where(
                lane < V_HEAD, h0, pltpu.roll(h1, V_HEAD, 1)).astype(o_ref.dtype)


def _flash_attention(q, k, v, *, batch, tq, heads_per_step, online):
    t = q.shape[0]
    seq = t // batch
    nq = seq // tq
    assert 2 * V_HEAD == HEAD_TILE and heads_per_step % 2 == 0 and N_HEADS % heads_per_step == 0
    q3, k3, v3 = (a.reshape(batch, seq, a.shape[1]) for a in (q, k, v))
    qi_list = [qi for qi in range(nq) for _ in range(qi + 1)]
    ki_list = [ki for qi in range(nq) for ki in range(qi + 1)]
    qi_tab = jnp.asarray(qi_list, jnp.int32)
    ki_tab = jnp.asarray(ki_list, jnp.int32)
    pair_tile = lambda tab: pl.BlockSpec((1, tq, heads_per_step * HEAD_TILE),
                                         lambda b, hp, s, qt, kt: (b, (qt if tab == "q" else kt)[s], hp))
    scratch = [pltpu.VMEM((heads_per_step, tq, HEAD_TILE), F32)]
    if online:
        scratch.append(pltpu.VMEM((heads_per_step, tq, 1), F32))
    grid_spec = pltpu.PrefetchScalarGridSpec(
        num_scalar_prefetch=2, grid=(batch, N_HEADS // heads_per_step, len(qi_list)),
        in_specs=[pair_tile("q"), pair_tile("k"), pair_tile("k")],
        out_specs=pl.BlockSpec((1, tq, heads_per_step * V_HEAD), lambda b, hp, s, qt, kt: (b, qt[s], hp)),
        scratch_shapes=scratch)
    out = pl.pallas_call(
        functools.partial(_flash_kernel, online=online), grid_spec=grid_spec,
        out_shape=jax.ShapeDtypeStruct((batch, seq, N_HEADS * V_HEAD), BF16),
        compiler_params=_compiler_params(3),
        name="flash_prompt_online" if online else "flash_prompt",
    )(qi_tab, ki_tab, q3, k3, v3)
    return out.reshape(t, N_HEADS * V_HEAD)


FETCH_SLOTS = 3
CHUNK_UNROLL = 1


def _paged_kernel(pt_ref, qa_ref, qpe_ref, latn_ref, kpen_ref, wukt_ref, lat_hbm, kpet_hbm,
                  o_ref, lat_buf, kpet_buf, sems, cb_ref, cbt_ref, cbn_ref, kpn_ref, s_ref, lhs_ref, *,
                  layer, n_pages, tk):
    b = pl.program_id(0)
    nb = pl.num_programs(0)
    past = n_pages * PAGE_SIZE
    n_chunks = past // tk
    t_new = latn_ref.shape[1]
    rows = qa_ref.shape[1]

    def page_copies(seq, slot, j):
        page = pt_ref[seq, j]
        dst = pl.ds(j * PAGE_SIZE, PAGE_SIZE)
        return (pltpu.make_async_copy(lat_hbm.at[layer, page], lat_buf.at[slot, dst], sems.at[slot, 0]),
                pltpu.make_async_copy(kpet_hbm.at[layer, page], kpet_buf.at[slot, j], sems.at[slot, 1]))

    def start_fetch(seq, slot, pages=None):
        for j in (range(n_pages) if pages is None else pages):
            lat_cp, kpe_cp = page_copies(seq, slot, j)
            lat_cp.start(priority=j % 2)
            kpe_cp.start(priority=(j + 1) % 2)

    def wait_fetch(slot):
        pltpu.make_async_copy(lat_buf.at[slot], lat_buf.at[slot], sems.at[slot, 0]).wait()
        pltpu.make_async_copy(kpet_buf.at[slot], kpet_buf.at[slot], sems.at[slot, 1]).wait()

    slot = b % FETCH_SLOTS

    @pl.when(b == 0)
    def _():
        for s in range(FETCH_SLOTS - 1):
            start_fetch(jnp.minimum(s, nb - 1), s)
        lhs_ref[0:N_HEADS * QK_NOPE, :] = wukt_ref[...]
        cbn_ref[...] = jnp.zeros_like(cbn_ref)
        kpn_ref[...] = jnp.zeros_like(kpn_ref)

    lhs_ref[N_HEADS * QK_NOPE:, :] = qa_ref[0]
    qpe = qpe_ref[0]
    pages_per_chunk = tk // PAGE_SIZE

    def scores(cbt, ss_pe, s_pe):
        r = _dot(lhs_ref[...], cbt)
        n = cbt.shape[1]
        kn = r[:N_HEADS * QK_NOPE]
        ss = jnp.sum((kn * kn).reshape(QK_NOPE, N_HEADS, n), axis=0) + ss_pe
        inv = lax.rsqrt(ss * (1.0 / QK_HEAD) + EPS)
        return (r[N_HEADS * QK_NOPE:] + s_pe) * jnp.concatenate([inv] * t_new, axis=0)

    def lane_tiles(a):
        return [a[:, i:i + LANES] for i in range(0, a.shape[1], LANES)]

    cbn_ref[0:t_new, :] = latn_ref[0].astype(BF16)
    kpn_ref[0:t_new, :] = kpen_ref[0]
    cbn = cbn_ref[...]
    cbn_t = cbn.astype(F32).T.astype(BF16)
    kpn_t = kpn_ref[...].T
    col = lax.broadcasted_iota(jnp.int32, (rows, LANES), 1)
    qrow = lax.broadcasted_iota(jnp.int32, (rows, LANES), 0) // N_HEADS
    causal_new = col <= qrow

    wait_fetch(slot)

    def chunk_group(g, m):
        for j in range(CHUNK_UNROLL):
            c = g * CHUNK_UNROLL + j
            st = pl.multiple_of(c * tk, tk)
            cb = lat_buf[slot, pl.ds(st, tk), :].astype(BF16)
            cb_ref[c, 0:tk, :] = cb
            cb_ref[c, tk:, :] = cbn
            kpt = jnp.concatenate([kpet_buf[slot, c * pages_per_chunk + i] for i in range(pages_per_chunk)]
                                  + [kpn_t], axis=1)
            cbt_ref[j, :, 0:tk] = cb.T
            cbt_ref[j, :, tk:] = cbn_t
            sc = scores(cbt_ref[j], jnp.sum(kpt * kpt, axis=0, keepdims=True), _dot(qpe, kpt.astype(BF16)))
            new_tile = jnp.where(jnp.logical_and(causal_new, c == n_chunks - 1), sc[:, tk:], NEG_BIG)
            sc = jnp.concatenate([sc[:, :tk], new_tile], axis=1)
            s_ref[c] = sc
            m = functools.reduce(jnp.maximum, [m] + lane_tiles(sc))
        return m

    m = lax.fori_loop(0, n_chunks // CHUNK_UNROLL, chunk_group, jnp.full((rows, LANES), NEG_BIG, F32))
    m = jnp.max(m, axis=-1, keepdims=True)

    ahead = FETCH_SLOTS - 1
    nxt = jnp.minimum(b + ahead, nb - 1)
    nxt_slot = (b + ahead) % FETCH_SLOTS
    pages_per_pv = n_pages // n_chunks
    l = jnp.zeros((rows, LANES), F32)
    o = jnp.zeros((rows, cb_ref.shape[2]), F32)
    for c in range(n_chunks):
        start_fetch(nxt, nxt_slot, range(c * pages_per_pv, (c + 1) * pages_per_pv))
        p = jnp.exp2(s_ref[c] - m)
        l = functools.reduce(jnp.add, [l] + lane_tiles(p))
        o = o + _dot(p.astype(BF16), cb_ref[c])
    l = jnp.sum(l, axis=-1, keepdims=True)
    o_ref[0] = (o / l).astype(o_ref.dtype)

    @pl.when(b == nb - 1)
    def _():
        for d in range(1, FETCH_SLOTS):
            wait_fetch((b + d) % FETCH_SLOTS)


def _paged_attention(page_table, qa, qpe, lat_new, kpe_new, cache_lat, cache_kpe_t, wukt, *, layer, tk):
    nb, n_pages = page_table.shape
    assert tk % PAGE_SIZE == 0 and cache_kpe_t.shape[2:] == (QK_ROPE, PAGE_SIZE)
    assert (n_pages * PAGE_SIZE) % (tk * CHUNK_UNROLL) == 0
    t_new = qa.shape[0] // nb
    kv_lora = cache_lat.shape[-1]
    past = n_pages * PAGE_SIZE
    rows = t_new * N_HEADS
    qa3 = qa.reshape(nb, rows, kv_lora)
    qpe3 = qpe.reshape(nb, rows, QK_ROPE)
    ln3 = lat_new.reshape(nb, t_new, kv_lora)
    kn3 = kpe_new.reshape(nb, t_new, QK_ROPE)
    per_seq = lambda r, w: pl.BlockSpec((1, r, w), lambda b, pt: (b, 0, 0))
    const = lambda a: pl.BlockSpec(a.shape, lambda b, pt: (0,) * a.ndim)
    grid_spec = pltpu.PrefetchScalarGridSpec(
        num_scalar_prefetch=1, grid=(nb,),
        in_specs=[per_seq(rows, kv_lora), per_seq(rows, QK_ROPE), per_seq(t_new, kv_lora),
                  per_seq(t_new, QK_ROPE), const(wukt), pl.BlockSpec(memory_space=pl.ANY),
                  pl.BlockSpec(memory_space=pl.ANY)],
        out_specs=per_seq(rows, kv_lora),
        scratch_shapes=[pltpu.VMEM((FETCH_SLOTS, past, kv_lora), F32),
                        pltpu.VMEM((FETCH_SLOTS, n_pages, QK_ROPE, PAGE_SIZE), F32),
                        pltpu.SemaphoreType.DMA((FETCH_SLOTS, 2)),
                        pltpu.VMEM((past // tk, tk + LANES, kv_lora), BF16),
                        pltpu.VMEM((CHUNK_UNROLL, kv_lora, tk + LANES), BF16),
                        pltpu.VMEM((LANES, kv_lora), BF16),
                        pltpu.VMEM((LANES, QK_ROPE), F32), pltpu.VMEM((past // tk, rows, tk + LANES), F32),
                        pltpu.VMEM((N_HEADS * QK_NOPE + rows, kv_lora), BF16)])
    kern = functools.partial(_paged_kernel, layer=layer, n_pages=n_pages, tk=tk)
    out = pl.pallas_call(
        kern, grid_spec=grid_spec,
        out_shape=jax.ShapeDtypeStruct((nb, rows, kv_lora), BF16),
        compiler_params=_compiler_params(1),
        name="paged_decode",
    )(page_table, qa3, qpe3, ln3, kn3, wukt, cache_lat, cache_kpe_t)
    return out.reshape(nb * t_new, N_HEADS * kv_lora)


FF_CHUNK = 256


def _out_ffn_kernel(x_ref, attn_ref, y_ref, gates_ref, wco_ref, womla_ref, wout_ref, gffn_ref,
                    wg_ref, wu_ref, wd_ref, *rest, latent_attn):
    if latent_attn:
        wuv_ref, o_ref, act_ref = rest
        attn = _dot(attn_ref[...], wuv_ref[...]).astype(BF16)
    else:
        o_ref, act_ref = rest
        attn = attn_ref[...]
    y_cur = y_ref[...]
    d_model = x_ref.shape[1]
    d_ff = wg_ref.shape[1]
    half = d_model // 2
    for c0 in (0, half):
        sl = slice(c0, c0 + half)
        merged = (gates_ref[:, sl] * _dot(y_cur, wco_ref[:, sl])
                  + gates_ref[:, d_model + c0:d_model + c0 + half] * _dot(attn, womla_ref[:, sl]))
        act_ref[:, sl] = merged.astype(BF16)
    o_ref[...] = x_ref[...] + _dot(act_ref[:, :d_model], wout_ref[...])
    x1 = o_ref[...]
    h2 = (x1 * _rms_scale(x1) * gffn_ref[...]).astype(BF16)
    for c0 in range(0, d_ff, FF_CHUNK):
        sl = slice(c0, c0 + FF_CHUNK)
        g = _dot(h2, wg_ref[:, sl])
        act_ref[:, sl] = (g * jax.nn.sigmoid(g) * _dot(h2, wu_ref[:, sl])).astype(BF16)
    o_ref[...] += _dot(act_ref[...], wd_ref[...])


def _out_ffn(x, attn, y_act, gates, wts, *, tm, latent_attn):
    t, d_model = x.shape
    d_ff = wts["w_gate"].shape[1]
    assert d_ff % FF_CHUNK == 0 and d_ff >= d_model
    row = lambda w: pl.BlockSpec((tm, w), lambda i: (i, 0))
    wspec = lambda a: _const_spec(a.shape, single_buffer=True)
    weights = [wts["w_conv_out"], wts["w_o_mla"], wts["w_out"], wts["g_ffn"], wts["w_gate"], wts["w_up"],
               wts["w_down"]] + ([wts["w_uv_bd"]] if latent_attn else [])
    return pl.pallas_call(
        functools.partial(_out_ffn_kernel, latent_attn=latent_attn), grid=(t // tm,),
        in_specs=[row(d_model), row(attn.shape[1]), row(y_act.shape[1]), row(2 * d_model)]
                 + [wspec(w) for w in weights],
        out_specs=row(d_model),
        out_shape=jax.ShapeDtypeStruct((t, d_model), F32),
        scratch_shapes=[pltpu.VMEM((tm, d_ff), BF16)],
        compiler_params=_compiler_params(1),
        name="out_ffn_decode" if latent_attn else "out_ffn_prompt",
    )(x, attn, y_act, gates, *weights)


def _head_tile_cols(w_rope, w_nope):
    lead = w_nope.shape[:-2]
    pad = jnp.zeros(lead + (N_HEADS, HEAD_TILE - QK_HEAD), w_nope.dtype)
    tile = jnp.concatenate([w_rope, w_nope, pad], axis=-1)
    return tile.reshape(lead + (N_HEADS * HEAD_TILE,))


def _gain_tile(g_compact):
    g = g_compact.astype(F32)
    return jnp.concatenate([g[QK_NOPE:], g[QK_NOPE:], g[:QK_NOPE], jnp.zeros((HEAD_TILE - QK_HEAD,), F32)])[None]


def _rope_tables(pos):
    inv_freq = 1.0 / (ROPE_THETA ** (jnp.arange(0, QK_ROPE, 2, dtype=F32) / QK_ROPE))
    freq_tile = jnp.concatenate([inv_freq, inv_freq, jnp.zeros((LANES - QK_ROPE,), F32)])
    ang = pos.astype(F32)[:, None] * freq_tile[None, :]
    c, s = jnp.cos(ang), jnp.sin(ang)
    lane = jnp.arange(LANES)[None, :]
    c_tab = jnp.where(lane < QK_HEAD, c, 0.0)
    sl_tab = jnp.where(lane < ROPE_HALF, -s, 0.0)
    sr_tab = jnp.where((lane >= ROPE_HALF) & (lane < QK_ROPE), s, 0.0)
    return c_tab, sl_tab, sr_tab


def _layer_weights(l, norm_mix_g, w_in, q_a_norm_g, w_uq, kv_a_norm_g, w_uk, w_uv, q_norm_g, k_norm_g,
                   w_o_mla, w_conv_out, w_out, norm_ffn_g, w_gate, w_up, w_down):
    d_model = w_in.shape[1]
    kv_lora = w_uk.shape[1]
    q_lora = w_uq.shape[1]
    d_conv = w_conv_out.shape[1]
    o1 = 2 * d_conv
    o2 = o1 + q_lora
    o3 = o2 + kv_lora
    o4 = o3 + QK_ROPE
    wi = w_in[l]
    w_in_p = jnp.concatenate([wi[:, :o3], wi[:, o3:o4], jnp.zeros((d_model, HEAD_TILE - QK_ROPE), wi.dtype),
                              wi[:, o4:]], axis=1).astype(BF16)
    uq = w_uq[l].reshape(q_lora, N_HEADS, QK_HEAD)
    w_uq_p = _head_tile_cols(uq[..., QK_NOPE:], uq[..., :QK_NOPE]).astype(BF16)
    uk = w_uk[l]
    w_uk_p = _head_tile_cols(jnp.zeros((kv_lora, N_HEADS, QK_ROPE), uk.dtype), uk).astype(BF16)
    w_ukt = jnp.transpose(uk, (2, 1, 0)).reshape(QK_NOPE * N_HEADS, kv_lora).astype(BF16)
    uv = w_uv[l]
    w_uv_p = jnp.concatenate([uv, jnp.zeros((kv_lora, N_HEADS, HEAD_TILE - V_HEAD), uv.dtype)],
                             axis=-1).reshape(kv_lora, N_HEADS * HEAD_TILE).astype(BF16)
    lane = jnp.arange(HEAD_TILE)
    bound = (1.02 * LOG2E * math.sqrt(QK_HEAD)) * jnp.max(jnp.abs(q_norm_g[l])) * jnp.max(jnp.abs(k_norm_g[l]))
    q_one = jnp.where(lane == BIAS_LANE, 1.0, 0.0).astype(F32)[None]
    k_bias = jnp.where(lane == BIAS_LANE, -bound, 0.0).astype(F32)[None]
    v_one = jnp.tile(jnp.where(lane == ONES_LANE, 1.0, 0.0).astype(F32), N_HEADS)[None]
    ukt = jnp.transpose(uk, (1, 2, 0))
    w_abs = jnp.concatenate([jnp.zeros((N_HEADS, QK_ROPE, kv_lora), uk.dtype), ukt,
                             jnp.zeros((N_HEADS, HEAD_TILE - QK_HEAD, kv_lora), uk.dtype)], axis=1).astype(BF16)
    lane_ids = jnp.arange(HEAD_TILE)[None, :, None]
    col_ids = jnp.arange(N_HEADS * QK_ROPE)[None, None, :]
    head_ids = jnp.arange(N_HEADS)[:, None, None]
    w_sel = ((lane_ids < QK_ROPE) & (col_ids == head_ids * QK_ROPE + lane_ids)).astype(BF16)
    eye_h = jnp.eye(N_HEADS, dtype=uv.dtype)
    w_uv_bd = jnp.einsum("chd,hg->hcgd", uv, eye_h).reshape(N_HEADS * kv_lora, N_HEADS * V_HEAD).astype(BF16)
    g_k = _gain_tile(k_norm_g[l])
    return dict(
        g_mix=norm_mix_g[l][None].astype(F32), w_in=w_in_p, g_qa=q_a_norm_g[l][None].astype(F32), w_uq=w_uq_p,
        g_kva=kv_a_norm_g[l][None].astype(F32), g_q=_gain_tile(q_norm_g[l]), g_k=g_k,
        w_uk=w_uk_p, w_uv_p=w_uv_p, w_uv_bd=w_uv_bd, q_one=q_one, k_bias=k_bias, v_one=v_one, score_bound=bound,
        w_ukt=w_ukt, w_abs=w_abs, w_sel=w_sel, w_conv_out=w_conv_out[l].astype(BF16),
        w_o_mla=w_o_mla[l].astype(BF16),
        w_out=w_out[l].astype(BF16), g_ffn=norm_ffn_g[l][None].astype(F32), w_gate=w_gate[l].astype(BF16),
        w_up=w_up[l].astype(BF16), w_down=w_down[l].astype(BF16))


ROW_TILE = 512
SUB_ROWS = 256
PAGED_CHUNK = 4096


def _tile_plan(seq, n_dec_tokens, past):
    row = min(ROW_TILE, seq)
    tk = min(PAGED_CHUNK, past)
    assert seq % row == 0 and row % SUB_ROWS == 0 and n_dec_tokens % 2 == 0
    assert past % (tk * CHUNK_UNROLL) == 0 and tk % PAGE_SIZE == 0
    return dict(row=row, sub=SUB_ROWS, dec=n_dec_tokens, dec_sub=n_dec_tokens // 2, tk=tk)


def kernel(x_prompt, x_sample, cache_kv_latent, cache_k_rope, state_conv, page_table, norm_mix_g, w_in,
           q_a_norm_g, w_uq, kv_a_norm_g, w_uk, w_uv, q_norm_g, k_norm_g, w_o_mla, conv_w, conv_b, conv_ln_g,
           conv_ln_b, w_conv_out, w_out, norm_ffn_g, w_gate, w_up, w_down):
    batch, seq, d_model = x_prompt.shape
    nb, t_new, _ = x_sample.shape
    depth = w_in.shape[0]
    n_pages = page_table.shape[1]
    past = n_pages * PAGE_SIZE
    n_state = state_conv.shape[2]
    d_conv = state_conv.shape[3]
    kv_lora = cache_kv_latent.shape[-1]

    plan = _tile_plan(seq, nb * t_new, past)
    tabs_p = _rope_tables(jnp.arange(seq))
    tabs_s = _rope_tables(jnp.tile(past + jnp.arange(t_new), nb))

    yp = x_prompt.reshape(batch * seq, d_model)
    ys = x_sample.reshape(nb * t_new, d_model)
    outs = [[] for _ in range(6)]
    for l in range(depth):
        wts = _layer_weights(l, norm_mix_g, w_in, q_a_norm_g, w_uq, kv_a_norm_g, w_uk, w_uv, q_norm_g,
                             k_norm_g, w_o_mla, w_conv_out, w_out, norm_ffn_g, w_gate, w_up, w_down)
        cb, lg, lb = (a[l][None].astype(F32) for a in (conv_b, conv_ln_g, conv_ln_b))

        u, lat, kpe, gates, q, k, v = _in_proj(yp, tabs_p, wts, tm=plan["row"], sub_rows=plan["sub"], decode=False)
        attn = lax.cond(wts["score_bound"] <= FIXED_SHIFT_MAX_BOUND,
                        functools.partial(_flash_attention, batch=batch, tq=plan["row"], heads_per_step=N_HEADS,
                                          online=False),
                        functools.partial(_flash_attention, batch=batch, tq=plan["row"], heads_per_step=2,
                                          online=True), q, k, v)
        y_act = _conv_prompt(u, conv_w[l].astype(F32), cb, lg, lb, batch=batch, ts=plan["row"])
        yp = _out_ffn(yp, attn, y_act, gates, wts, tm=plan["row"], latent_attn=False)
        outs[0].append(lat.reshape(batch, seq, kv_lora))
        outs[1].append(kpe.reshape(batch, seq, QK_ROPE))
        outs[2].append(u.reshape(batch, seq, d_conv)[:, seq - n_state:])

        u, lat, kpe, gates, qa, qpe = _in_proj(ys, tabs_s, wts, tm=plan["dec"], sub_rows=plan["dec_sub"], decode=True)
        u_t = jnp.transpose(u.reshape(nb, t_new, d_conv), (1, 0, 2))
        state_t = jnp.transpose(state_conv[l], (1, 0, 2))
        y_act = _conv_decode(state_t, u_t, conv_w[l].astype(F32), cb, lg, lb)
        y_act = jnp.transpose(y_act, (1, 0, 2)).reshape(nb * t_new, d_conv)
        attn = _paged_attention(page_table, qa, qpe, lat, kpe, cache_kv_latent, jnp.swapaxes(cache_k_rope, 2, 3),
                                wts["w_ukt"], layer=l, tk=plan["tk"])
        ys = _out_ffn(ys, attn, y_act, gates, wts, tm=plan["dec"], latent_attn=True)
        outs[3].append(lat.reshape(nb, t_new, kv_lora))
        outs[4].append(kpe.reshape(nb, t_new, QK_ROPE))
        outs[5].append(jnp.transpose(jnp.concatenate([state_t[t_new:], u_t], axis=0), (1, 0, 2)))

    return (yp.reshape(batch, seq, d_model), ys.reshape(nb, t_new, d_model),
            *(jnp.stack(o) for o in outs))
```

```python
import functools
import math

import jax
import jax.numpy as jnp
from jax import lax
from jax.experimental import pallas as pl
from jax.experimental.pallas import tpu as pltpu

N_HEADS = 8
QK_NOPE = 64
QK_ROPE = 32
ROPE_HALF = QK_ROPE // 2
QK_HEAD = QK_NOPE + QK_ROPE
V_HEAD = 64
ROPE_THETA = 10000.0
EPS = 1e-6
PAGE_SIZE = 128

LANES = 128
HEAD_TILE = LANES
NOPE_OFF = QK_ROPE
VMEM_LIMIT_BYTES = 56 * 1024 * 1024

BIAS_LANE = QK_HEAD
ONES_LANE = V_HEAD
LOG2E = 1.4426950408889634
FIXED_SHIFT_MAX_BOUND = 40.0

F32 = jnp.float32
BF16 = jnp.bfloat16
NEG_BIG = -1e30


def _dot(a, b):
    return jnp.dot(a, b, preferred_element_type=F32)


def _dot_nt(a, b):
    return lax.dot_general(a, b, (((1,), (1,)), ((), ())), preferred_element_type=F32)


def _rms_scale(x):
    return lax.rsqrt(jnp.mean(x * x, axis=-1, keepdims=True) + EPS)


def _compiler_params(n_grid_axes):
    return pltpu.CompilerParams(dimension_semantics=("arbitrary",) * n_grid_axes,
                                vmem_limit_bytes=VMEM_LIMIT_BYTES)


def _const_spec(shape, single_buffer=False):
    nd = len(shape)
    if single_buffer:
        return pl.BlockSpec(shape, lambda *_: (0,) * nd, pipeline_mode=pl.Buffered(1))
    return pl.BlockSpec(shape, lambda *_: (0,) * nd)


def _rope_tile(t, c_tab, sl_tab, sr_tab):
    return t * c_tab + pltpu.roll(t, ROPE_HALF, 1) * sr_tab + pltpu.roll(t, LANES - ROPE_HALF, 1) * sl_tab


N_ROW_INPUTS = 4
N_WEIGHT_INPUTS = 14
KPE_OUT = 2


def _in_proj_kernel(*refs, sub_rows, **kw):
    n_in = N_ROW_INPUTS + N_WEIGHT_INPUTS
    rows, weights, outs = refs[:N_ROW_INPUTS], refs[N_ROW_INPUTS:n_in], list(refs[n_in:])
    for r0 in range(0, rows[0].shape[0], sub_rows):
        view = lambda ref: ref.at[pl.ds(r0, sub_rows)]
        lane_view = lambda ref: ref.at[0, :, pl.ds(r0, sub_rows)]
        sub_outs = [lane_view(o) if (n == KPE_OUT and not kw["decode"]) else view(o) for n, o in enumerate(outs)]
        _in_proj_rows(*map(view, rows), *weights, *sub_outs, **kw)


def _in_proj_rows(x_ref, ctab_ref, sltab_ref, srtab_ref, gmix_ref, w_in_ref, gqa_ref, w_uq_ref,
                  gkva_ref, gq_ref, gk_ref, qone_ref, kbias_ref, vone_ref, w_uk_ref, w_uv_ref,
                  wabs_ref, sel_ref, u_ref, lat_ref, kpe_ref, gates_ref, *attn_refs,
                  d_conv, q_lora, kv_lora, d_model, decode):
    x = x_ref[...]
    hb = (x * _rms_scale(x) * gmix_ref[...]).astype(BF16)
    c_tab, sl_tab, sr_tab = ctab_ref[...], sltab_ref[...], srtab_ref[...]

    o_cq = 2 * d_conv
    o_ckv = o_cq + q_lora
    o_kpe = o_ckv + kv_lora
    o_gate = o_kpe + HEAD_TILE

    ag = _dot(hb, w_in_ref[:, 0:o_cq])
    u_ref[...] = ag[:, :d_conv] * jax.nn.sigmoid(ag[:, d_conv:])

    cq = _dot(hb, w_in_ref[:, o_cq:o_ckv])
    cqn = (cq * _rms_scale(cq) * gqa_ref[...]).astype(BF16)
    q = _dot(cqn, w_uq_ref[...])
    q_scale = LOG2E / math.sqrt(QK_HEAD)
    qpe = None
    for h in range(N_HEADS):
        sl = slice(h * HEAD_TILE, (h + 1) * HEAD_TILE)
        qh = _rope_tile(q[:, sl], c_tab, sl_tab, sr_tab)
        ss = jnp.sum(qh * qh, axis=-1, keepdims=True)
        qh = qh * lax.rsqrt(ss * (1.0 / QK_HEAD) + EPS) * (gq_ref[...] * q_scale)
        if decode:
            qa_ref, qpe_ref = attn_refs
            qg = (qh * gk_ref[...]).astype(BF16)
            qa_ref[:, h * kv_lora:(h + 1) * kv_lora] = _dot(qg, wabs_ref[h]).astype(qa_ref.dtype)
            part = _dot(qg, sel_ref[h])
            qpe = part if qpe is None else qpe + part
        else:
            attn_refs[0][:, sl] = (qh + qone_ref[...]).astype(attn_refs[0].dtype)
    if decode:
        qpe_ref[...] = qpe.astype(qpe_ref.dtype)

    ck = _dot(hb, w_in_ref[:, o_ckv:o_gate])
    ckv = ck[:, :kv_lora]
    lat = ckv * _rms_scale(ckv) * gkva_ref[...]
    lat_ref[...] = lat
    kpe = _rope_tile(ck[:, kv_lora:], c_tab, sl_tab, sr_tab)
    if decode:
        kpe_ref[...] = kpe[:, :QK_ROPE]
    else:
        kpe_ref[...] = kpe.T[:QK_ROPE]

    for g in range(2):
        sl = slice(g * d_model, (g + 1) * d_model)
        gate = jax.nn.sigmoid(_dot(hb, w_in_ref[:, o_gate + g * d_model:o_gate + (g + 1) * d_model]))
        gates_ref[:, sl] = gate.astype(gates_ref.dtype)

    if not decode:
        _, k_ref, v_ref = attn_refs
        latb = lat.astype(BF16)
        v_ref[...] = (_dot(latb, w_uv_ref[...]) + vone_ref[...]).astype(v_ref.dtype)
        kn = _dot(latb, w_uk_ref[...])
        for h in range(N_HEADS):
            sl = slice(h * HEAD_TILE, (h + 1) * HEAD_TILE)
            kh = kn[:, sl] + kpe
            ss = jnp.sum(kh * kh, axis=-1, keepdims=True)
            kh = kh * lax.rsqrt(ss * (1.0 / QK_HEAD) + EPS) * gk_ref[...] + kbias_ref[...]
            k_ref[:, sl] = kh.astype(k_ref.dtype)


def _in_proj(x, tabs, wts, *, tm, sub_rows, decode):
    t, d_model = x.shape
    n_tab_tiles = tabs[0].shape[0] // tm
    d_in_p = wts["w_in"].shape[1]
    q_lora = wts["w_uq"].shape[0]
    kv_lora = wts["w_uk"].shape[0]
    d_conv = (d_in_p - q_lora - kv_lora - HEAD_TILE - 2 * d_model) // 2
    qw = N_HEADS * HEAD_TILE
    vw = N_HEADS * HEAD_TILE

    row = lambda w: pl.BlockSpec((tm, w), lambda i: (i, 0))
    tab = pl.BlockSpec((tm, LANES), lambda i: (i % n_tab_tiles, 0))
    in_specs = [row(d_model), tab, tab, tab,
                _const_spec((1, d_model)), _const_spec(wts["w_in"].shape),
                _const_spec((1, q_lora)), _const_spec(wts["w_uq"].shape),
                _const_spec((1, kv_lora)), _const_spec((1, HEAD_TILE)), _const_spec((1, HEAD_TILE)),
                _const_spec((1, HEAD_TILE)), _const_spec((1, HEAD_TILE)), _const_spec((1, vw)),
                _const_spec(wts["w_uk"].shape), _const_spec(wts["w_uv_p"].shape),
                _const_spec(wts["w_abs"].shape), _const_spec(wts["w_sel"].shape)]
    out_shape = [jax.ShapeDtypeStruct((t, d_conv), F32), jax.ShapeDtypeStruct((t, kv_lora), F32),
                 jax.ShapeDtypeStruct((t, QK_ROPE), F32), jax.ShapeDtypeStruct((t, 2 * d_model), BF16)]
    out_specs = [row(d_conv), row(kv_lora), row(QK_ROPE), row(2 * d_model)]
    if not decode:
        out_shape[KPE_OUT] = jax.ShapeDtypeStruct((t // (n_tab_tiles * tm), QK_ROPE, n_tab_tiles * tm), F32)
        out_specs[KPE_OUT] = pl.BlockSpec((1, QK_ROPE, tm), lambda i: (i // n_tab_tiles, 0, i % n_tab_tiles))
    if decode:
        widths = [N_HEADS * kv_lora, N_HEADS * QK_ROPE]
    else:
        widths = [qw, qw, vw]
    out_shape += [jax.ShapeDtypeStruct((t, w), BF16) for w in widths]
    out_specs += [row(w) for w in widths]
    kern = functools.partial(_in_proj_kernel, d_conv=d_conv, q_lora=q_lora, kv_lora=kv_lora,
                             d_model=d_model, decode=decode, sub_rows=sub_rows)
    return pl.pallas_call(
        kern, grid=(t // tm,), in_specs=in_specs, out_specs=out_specs, out_shape=out_shape,
        compiler_params=_compiler_params(1),
        name="in_proj_decode" if decode else "in_proj_prompt",
    )(x, *tabs, wts["g_mix"], wts["w_in"], wts["g_qa"], wts["w_uq"], wts["g_kva"], wts["g_q"], wts["g_k"],
      wts["q_one"], wts["k_bias"], wts["v_one"], wts["w_uk"], wts["w_uv_p"], wts["w_abs"], wts["w_sel"])


def _ln_silu(y, g, b):
    mu = jnp.mean(y, axis=-1, keepdims=True)
    yc = y - mu
    yn = yc * lax.rsqrt(jnp.mean(yc * yc, axis=-1, keepdims=True) + EPS) * g + b
    return yn * jax.nn.sigmoid(yn)


HALO = 32


SUBLANES = 8


def _conv_prompt_kernel(cur_ref, prev_ref, w_ref, b_ref, lng_ref, lnb_ref, y_ref, ext_ref, sh_ref, *, rows):
    ts = cur_ref.shape[1]
    width = w_ref.shape[0]
    ext_ref[HALO:, :] = cur_ref[0]
    prev = prev_ref[0]
    ext_ref[0:HALO, :] = jnp.where(pl.program_id(1) > 0, prev, jnp.zeros_like(prev))
    span = sh_ref.shape[1]
    for r in range(1, SUBLANES):
        sh_ref[r - 1] = ext_ref[r:r + span, :]
    base = HALO - (width - 1)
    for r0 in range(0, ts, rows):
        acc = None
        for k in range(width):
            kk = base + k
            start = r0 + (kk // SUBLANES) * SUBLANES
            if kk % SUBLANES == 0:
                window = ext_ref[start:start + rows, :]
            else:
                window = sh_ref[kk % SUBLANES - 1, start:start + rows, :]
            term = window * jnp.tile(w_ref[k], (rows // SUBLANES, 1))
            acc = term if acc is None else acc + term
        y = _ln_silu(acc + b_ref[...], lng_ref[...], lnb_ref[...])
        y_ref[r0:r0 + rows, :] = y.astype(y_ref.dtype)


def _conv_prompt(u, conv_w, conv_b, ln_g, ln_b, *, batch, ts, rows=64):
    t, d_conv = u.shape
    seq = t // batch
    width = conv_w.shape[0]
    assert width - 1 <= HALO and ts % HALO == 0 and seq % ts == 0 and HALO % SUBLANES == 0
    u3 = u.reshape(batch, seq, d_conv)
    nst = seq // ts
    span = ts + HALO - SUBLANES
    conv_w = jnp.broadcast_to(conv_w[:, None, :], (width, SUBLANES, d_conv))
    kern = functools.partial(_conv_prompt_kernel, rows=rows)
    return pl.pallas_call(
        kern, grid=(batch, nst),
        in_specs=[pl.BlockSpec((1, ts, d_conv), lambda b, i: (b, i, 0)),
                  pl.BlockSpec((1, HALO, d_conv), lambda b, i: (b, jnp.maximum(i * (ts // HALO) - 1, 0), 0)),
                  _const_spec(conv_w.shape), _const_spec((1, d_conv)), _const_spec((1, d_conv)),
                  _const_spec((1, d_conv))],
        out_specs=pl.BlockSpec((ts, d_conv), lambda b, i: (b * nst + i, 0)),
        out_shape=jax.ShapeDtypeStruct((t, d_conv), BF16),
        scratch_shapes=[pltpu.VMEM((ts + HALO, d_conv), F32), pltpu.VMEM((SUBLANES - 1, span, d_conv), F32)],
        compiler_params=_compiler_params(2),
        name="conv_prompt",
    )(u3, u3, conv_w, conv_b, ln_g, ln_b)


def _conv_decode_kernel(state_ref, unew_ref, w_ref, b_ref, lng_ref, lnb_ref, y_ref):
    n_state = state_ref.shape[0]
    for t in range(unew_ref.shape[0]):
        acc = None
        for k in range(w_ref.shape[0]):
            j = t + k
            src = state_ref[j] if j < n_state else unew_ref[j - n_state]
            term = src * w_ref[k:k + 1, :]
            acc = term if acc is None else acc + term
        y_ref[t] = _ln_silu(acc + b_ref[...], lng_ref[...], lnb_ref[...]).astype(y_ref.dtype)


def _conv_decode(state_t, u_new_t, conv_w, conv_b, ln_g, ln_b):
    t_new, nb, d_conv = u_new_t.shape
    assert state_t.shape[0] == conv_w.shape[0] - 1
    full = lambda a: _const_spec(a.shape)
    return pl.pallas_call(
        _conv_decode_kernel, grid=(1,),
        in_specs=[full(state_t), full(u_new_t), full(conv_w), _const_spec((1, d_conv)),
                  _const_spec((1, d_conv)), _const_spec((1, d_conv))],
        out_specs=_const_spec((t_new, nb, d_conv)),
        out_shape=jax.ShapeDtypeStruct((t_new, nb, d_conv), BF16),
        compiler_params=_compiler_params(1),
        name="conv_decode",
    )(state_t, u_new_t, conv_w, conv_b, ln_g, ln_b)


def _flash_kernel(qi_tab, ki_tab, q_ref, k_ref, v_ref, o_ref, acc_ref, *m_refs, online):
    step = pl.program_id(2)
    qi = qi_tab[step]
    ki = ki_tab[step]
    tq = q_ref.shape[1]
    tk = k_ref.shape[1]
    n_heads = acc_ref.shape[0]

    @pl.when(ki == 0)
    def _():
        acc_ref[...] = jnp.zeros_like(acc_ref)
        if online:
            m_refs[0][...] = jnp.full_like(m_refs[0], NEG_BIG)

    def accumulate(masked):
        if masked:
            visible = (lax.broadcasted_iota(jnp.int32, (tq, tk), 1)
                       <= lax.broadcasted_iota(jnp.int32, (tq, tk), 0))
        for hh in range(n_heads):
            sl = slice(hh * HEAD_TILE, (hh + 1) * HEAD_TILE)
            s = _dot_nt(q_ref[0, :, sl], k_ref[0, :, sl])
            if masked:
                s = jnp.where(visible, s, NEG_BIG)
            if online:
                m_old = m_refs[0][hh]
                m_new = jnp.maximum(m_old, jnp.max(s, axis=-1, keepdims=True))
                m_refs[0][hh] = m_new
                p = jnp.exp2(s - m_new).astype(BF16)
                acc_ref[hh] = jnp.exp2(m_old - m_new) * acc_ref[hh] + _dot(p, v_ref[0, :, sl])
            else:
                acc_ref[hh] += _dot(jnp.exp2(s).astype(BF16), v_ref[0, :, sl])

    @pl.when(ki < qi)
    def _():
        accumulate(False)

    @pl.when(ki == qi)
    def _():
        accumulate(True)
        lane = lax.broadcasted_iota(jnp.int32, (tq, HEAD_TILE), 1)
        for pair in range(n_heads // 2):
            a0, a1 = acc_ref[2 * pair], acc_ref[2 * pair + 1]
            h0 = a0 / a0[:, ONES_LANE:ONES_LANE + 1]
            h1 = a1 / a1[:, ONES_LANE:ONES_LANE + 1]
            o_ref[0, :, pair * HEAD_TILE:(pair + 1) * HEAD_TILE] = jnp.where(
                lane < V_HEAD, h0, pltpu.roll(h1, V_HEAD, 1)).astype(o_ref.dtype)


def _flash_attention(q, k, v, *, batch, tq, heads_per_step, online):
    t = q.shape[0]
    seq = t // batch
    nq = seq // tq
    assert 2 * V_HEAD == HEAD_TILE and heads_per_step % 2 == 0 and N_HEADS % heads_per_step == 0
    q3, k3, v3 = (a.reshape(batch, seq, a.shape[1]) for a in (q, k, v))
    qi_list = [qi for qi in range(nq) for _ in range(qi + 1)]
    ki_list = [ki for qi in range(nq) for ki in range(qi + 1)]
    qi_tab = jnp.asarray(qi_list, jnp.int32)
    ki_tab = jnp.asarray(ki_list, jnp.int32)
    pair_tile = lambda tab: pl.BlockSpec((1, tq, heads_per_step * HEAD_TILE),
                                         lambda b, hp, s, qt, kt: (b, (qt if tab == "q" else kt)[s], hp))
    scratch = [pltpu.VMEM((heads_per_step, tq, HEAD_TILE), F32)]
    if online:
        scratch.append(pltpu.VMEM((heads_per_step, tq, 1), F32))
    grid_spec = pltpu.PrefetchScalarGridSpec(
        num_scalar_prefetch=2, grid=(batch, N_HEADS // heads_per_step, len(qi_list)),
        in_specs=[pair_tile("q"), pair_tile("k"), pair_tile("k")],
        out_specs=pl.BlockSpec((1, tq, heads_per_step * V_HEAD), lambda b, hp, s, qt, kt: (b, qt[s], hp)),
        scratch_shapes=scratch)
    out = pl.pallas_call(
        functools.partial(_flash_kernel, online=online), grid_spec=grid_spec,
        out_shape=jax.ShapeDtypeStruct((batch, seq, N_HEADS * V_HEAD), BF16),
        compiler_params=_compiler_params(3),
        name="flash_prompt_online" if online else "flash_prompt",
    )(qi_tab, ki_tab, q3, k3, v3)
    return out.reshape(t, N_HEADS * V_HEAD)


FETCH_SLOTS = 3
CHUNK_UNROLL = 1


def _paged_kernel(pt_ref, qa_ref, qpe_ref, latn_ref, kpen_ref, wukt_ref, lat_hbm, kpet_hbm,
                  o_ref, lat_buf, kpet_buf, sems, cb_ref, cbt_ref, cbn_ref, kpn_ref, s_ref, lhs_ref, *,
                  layer, n_pages, tk):
    b = pl.program_id(0)
    nb = pl.num_programs(0)
    past = n_pages * PAGE_SIZE
    n_chunks = past // tk
    t_new = latn_ref.shape[1]
    rows = qa_ref.shape[1]

    def page_copies(seq, slot, j):
        page = pt_ref[seq, j]
        dst = pl.ds(j * PAGE_SIZE, PAGE_SIZE)
        return (pltpu.make_async_copy(lat_hbm.at[layer, page], lat_buf.at[slot, dst], sems.at[slot, 0]),
                pltpu.make_async_copy(kpet_hbm.at[layer, page], kpet_buf.at[slot, j], sems.at[slot, 1]))

    def start_fetch(seq, slot, pages=None):
        for j in (range(n_pages) if pages is None else pages):
            lat_cp, kpe_cp = page_copies(seq, slot, j)
            lat_cp.start(priority=j % 2)
            kpe_cp.start(priority=(j + 1) % 2)

    def wait_fetch(slot):
        pltpu.make_async_copy(lat_buf.at[slot], lat_buf.at[slot], sems.at[slot, 0]).wait()
        pltpu.make_async_copy(kpet_buf.at[slot], kpet_buf.at[slot], sems.at[slot, 1]).wait()

    slot = b % FETCH_SLOTS

    @pl.when(b == 0)
    def _():
        for s in range(FETCH_SLOTS - 1):
            start_fetch(jnp.minimum(s, nb - 1), s)
        lhs_ref[0:N_HEADS * QK_NOPE, :] = wukt_ref[...]
        cbn_ref[...] = jnp.zeros_like(cbn_ref)
        kpn_ref[...] = jnp.zeros_like(kpn_ref)

    lhs_ref[N_HEADS * QK_NOPE:, :] = qa_ref[0]
    qpe = qpe_ref[0]
    pages_per_chunk = tk // PAGE_SIZE

    def scores(cbt, ss_pe, s_pe):
        r = _dot(lhs_ref[...], cbt)
        n = cbt.shape[1]
        kn = r[:N_HEADS * QK_NOPE]
        ss = jnp.sum((kn * kn).reshape(QK_NOPE, N_HEADS, n), axis=0) + ss_pe
        inv = lax.rsqrt(ss * (1.0 / QK_HEAD) + EPS)
        return (r[N_HEADS * QK_NOPE:] + s_pe) * jnp.concatenate([inv] * t_new, axis=0)

    def lane_tiles(a):
        return [a[:, i:i + LANES] for i in range(0, a.shape[1], LANES)]

    cbn_ref[0:t_new, :] = latn_ref[0].astype(BF16)
    kpn_ref[0:t_new, :] = kpen_ref[0]
    cbn = cbn_ref[...]
    cbn_t = cbn.astype(F32).T.astype(BF16)
    kpn_t = kpn_ref[...].T
    col = lax.broadcasted_iota(jnp.int32, (rows, LANES), 1)
    qrow = lax.broadcasted_iota(jnp.int32, (rows, LANES), 0) // N_HEADS
    causal_new = col <= qrow

    wait_fetch(slot)

    def chunk_group(g, m):
        for j in range(CHUNK_UNROLL):
            c = g * CHUNK_UNROLL + j
            st = pl.multiple_of(c * tk, tk)
            cb = lat_buf[slot, pl.ds(st, tk), :].astype(BF16)
            cb_ref[c, 0:tk, :] = cb
            cb_ref[c, tk:, :] = cbn
            kpt = jnp.concatenate([kpet_buf[slot, c * pages_per_chunk + i] for i in range(pages_per_chunk)]
                                  + [kpn_t], axis=1)
            cbt_ref[j, :, 0:tk] = cb.T
            cbt_ref[j, :, tk:] = cbn_t
            sc = scores(cbt_ref[j], jnp.sum(kpt * kpt, axis=0, keepdims=True), _dot(qpe, kpt.astype(BF16)))
            new_tile = jnp.where(jnp.logical_and(causal_new, c == n_chunks - 1), sc[:, tk:], NEG_BIG)
            sc = jnp.concatenate([sc[:, :tk], new_tile], axis=1)
            s_ref[c] = sc
            m = functools.reduce(jnp.maximum, [m] + lane_tiles(sc))
        return m

    m = lax.fori_loop(0, n_chunks // CHUNK_UNROLL, chunk_group, jnp.full((rows, LANES), NEG_BIG, F32))
    m = jnp.max(m, axis=-1, keepdims=True)

    ahead = FETCH_SLOTS - 1
    nxt = jnp.minimum(b + ahead, nb - 1)
    nxt_slot = (b + ahead) % FETCH_SLOTS
    pages_per_pv = n_pages // n_chunks
    l = jnp.zeros((rows, LANES), F32)
    o = jnp.zeros((rows, cb_ref.shape[2]), F32)
    for c in range(n_chunks):
        start_fetch(nxt, nxt_slot, range(c * pages_per_pv, (c + 1) * pages_per_pv))
        p = jnp.exp2(s_ref[c] - m)
        l = functools.reduce(jnp.add, [l] + lane_tiles(p))
        o = o + _dot(p.astype(BF16), cb_ref[c])
    l = jnp.sum(l, axis=-1, keepdims=True)
    o_ref[0] = (o / l).astype(o_ref.dtype)

    @pl.when(b == nb - 1)
    def _():
        for d in range(1, FETCH_SLOTS):
            wait_fetch((b + d) % FETCH_SLOTS)


def _paged_attention(page_table, qa, qpe, lat_new, kpe_new, cache_lat, cache_kpe_t, wukt, *, layer, tk):
    nb, n_pages = page_table.shape
    assert tk % PAGE_SIZE == 0 and cache_kpe_t.shape[2:] == (QK_ROPE, PAGE_SIZE)
    assert (n_pages * PAGE_SIZE) % (tk * CHUNK_UNROLL) == 0
    t_new = qa.shape[0] // nb
    kv_lora = cache_lat.shape[-1]
    past = n_pages * PAGE_SIZE
    rows = t_new * N_HEADS
    qa3 = qa.reshape(nb, rows, kv_lora)
    qpe3 = qpe.reshape(nb, rows, QK_ROPE)
    ln3 = lat_new.reshape(nb, t_new, kv_lora)
    kn3 = kpe_new.reshape(nb, t_new, QK_ROPE)
    per_seq = lambda r, w: pl.BlockSpec((1, r, w), lambda b, pt: (b, 0, 0))
    const = lambda a: pl.BlockSpec(a.shape, lambda b, pt: (0,) * a.ndim)
    grid_spec = pltpu.PrefetchScalarGridSpec(
        num_scalar_prefetch=1, grid=(nb,),
        in_specs=[per_seq(rows, kv_lora), per_seq(rows, QK_ROPE), per_seq(t_new, kv_lora),
                  per_seq(t_new, QK_ROPE), const(wukt), pl.BlockSpec(memory_space=pl.ANY),
                  pl.BlockSpec(memory_space=pl.ANY)],
        out_specs=per_seq(rows, kv_lora),
        scratch_shapes=[pltpu.VMEM((FETCH_SLOTS, past, kv_lora), F32),
                        pltpu.VMEM((FETCH_SLOTS, n_pages, QK_ROPE, PAGE_SIZE), F32),
                        pltpu.SemaphoreType.DMA((FETCH_SLOTS, 2)),
                        pltpu.VMEM((past // tk, tk + LANES, kv_lora), BF16),
                        pltpu.VMEM((CHUNK_UNROLL, kv_lora, tk + LANES), BF16),
                        pltpu.VMEM((LANES, kv_lora), BF16),
                        pltpu.VMEM((LANES, QK_ROPE), F32), pltpu.VMEM((past // tk, rows, tk + LANES), F32),
                        pltpu.VMEM((N_HEADS * QK_NOPE + rows, kv_lora), BF16)])
    kern = functools.partial(_paged_kernel, layer=layer, n_pages=n_pages, tk=tk)
    out = pl.pallas_call(
        kern, grid_spec=grid_spec,
        out_shape=jax.ShapeDtypeStruct((nb, rows, kv_lora), BF16),
        compiler_params=_compiler_params(1),
        name="paged_decode",
    )(page_table, qa3, qpe3, ln3, kn3, wukt, cache_lat, cache_kpe_t)
    return out.reshape(nb * t_new, N_HEADS * kv_lora)


FF_CHUNK = 256


def _out_ffn_kernel(x_ref, attn_ref, y_ref, gates_ref, wco_ref, womla_ref, wout_ref, gffn_ref,
                    wg_ref, wu_ref, wd_ref, *rest, latent_attn):
    if latent_attn:
        wuv_ref, o_ref, act_ref = rest
        attn = _dot(attn_ref[...], wuv_ref[...]).astype(BF16)
    else:
        o_ref, act_ref = rest
        attn = attn_ref[...]
    y_cur = y_ref[...]
    d_model = x_ref.shape[1]
    d_ff = wg_ref.shape[1]
    half = d_model // 2
    for c0 in (0, half):
        sl = slice(c0, c0 + half)
        merged = (gates_ref[:, sl] * _dot(y_cur, wco_ref[:, sl])
                  + gates_ref[:, d_model + c0:d_model + c0 + half] * _dot(attn, womla_ref[:, sl]))
        act_ref[:, sl] = merged.astype(BF16)
    o_ref[...] = x_ref[...] + _dot(act_ref[:, :d_model], wout_ref[...])
    x1 = o_ref[...]
    h2 = (x1 * _rms_scale(x1) * gffn_ref[...]).astype(BF16)
    for c0 in range(0, d_ff, FF_CHUNK):
        sl = slice(c0, c0 + FF_CHUNK)
        g = _dot(h2, wg_ref[:, sl])
        act_ref[:, sl] = (g * jax.nn.sigmoid(g) * _dot(h2, wu_ref[:, sl])).astype(BF16)
    o_ref[...] += _dot(act_ref[...], wd_ref[...])


def _out_ffn(x, attn, y_act, gates, wts, *, tm, latent_attn):
    t, d_model = x.shape
    d_ff = wts["w_gate"].shape[1]
    assert d_ff % FF_CHUNK == 0 and d_ff >= d_model
    row = lambda w: pl.BlockSpec((tm, w), lambda i: (i, 0))
    wspec = lambda a: _const_spec(a.shape, single_buffer=True)
    weights = [wts["w_conv_out"], wts["w_o_mla"], wts["w_out"], wts["g_ffn"], wts["w_gate"], wts["w_up"],
               wts["w_down"]] + ([wts["w_uv_bd"]] if latent_attn else [])
    return pl.pallas_call(
        functools.partial(_out_ffn_kernel, latent_attn=latent_attn), grid=(t // tm,),
        in_specs=[row(d_model), row(attn.shape[1]), row(y_act.shape[1]), row(2 * d_model)]
                 + [wspec(w) for w in weights],
        out_specs=row(d_model),
        out_shape=jax.ShapeDtypeStruct((t, d_model), F32),
        scratch_shapes=[pltpu.VMEM((tm, d_ff), BF16)],
        compiler_params=_compiler_params(1),
        name="out_ffn_decode" if latent_attn else "out_ffn_prompt",
    )(x, attn, y_act, gates, *weights)


def _head_tile_cols(w_rope, w_nope):
    lead = w_nope.shape[:-2]
    pad = jnp.zeros(lead + (N_HEADS, HEAD_TILE - QK_HEAD), w_nope.dtype)
    tile = jnp.concatenate([w_rope, w_nope, pad], axis=-1)
    return tile.reshape(lead + (N_HEADS * HEAD_TILE,))


def _gain_tile(g_compact):
    g = g_compact.astype(F32)
    return jnp.concatenate([g[QK_NOPE:], g[QK_NOPE:], g[:QK_NOPE], jnp.zeros((HEAD_TILE - QK_HEAD,), F32)])[None]


def _rope_tables(pos):
    inv_freq = 1.0 / (ROPE_THETA ** (jnp.arange(0, QK_ROPE, 2, dtype=F32) / QK_ROPE))
    freq_tile = jnp.concatenate([inv_freq, inv_freq, jnp.zeros((LANES - QK_ROPE,), F32)])
    ang = pos.astype(F32)[:, None] * freq_tile[None, :]
    c, s = jnp.cos(ang), jnp.sin(ang)
    lane = jnp.arange(LANES)[None, :]
    c_tab = jnp.where(lane < QK_HEAD, c, 0.0)
    sl_tab = jnp.where(lane < ROPE_HALF, -s, 0.0)
    sr_tab = jnp.where((lane >= ROPE_HALF) & (lane < QK_ROPE), s, 0.0)
    return c_tab, sl_tab, sr_tab


def _layer_weights(l, norm_mix_g, w_in, q_a_norm_g, w_uq, kv_a_norm_g, w_uk, w_uv, q_norm_g, k_norm_g,
                   w_o_mla, w_conv_out, w_out, norm_ffn_g, w_gate, w_up, w_down):
    d_model = w_in.shape[1]
    kv_lora = w_uk.shape[1]
    q_lora = w_uq.shape[1]
    d_conv = w_conv_out.shape[1]
    o1 = 2 * d_conv
    o2 = o1 + q_lora
    o3 = o2 + kv_lora
    o4 = o3 + QK_ROPE
    wi = w_in[l]
    w_in_p = jnp.concatenate([wi[:, :o3], wi[:, o3:o4], jnp.zeros((d_model, HEAD_TILE - QK_ROPE), wi.dtype),
                              wi[:, o4:]], axis=1).astype(BF16)
    uq = w_uq[l].reshape(q_lora, N_HEADS, QK_HEAD)
    w_uq_p = _head_tile_cols(uq[..., QK_NOPE:], uq[..., :QK_NOPE]).astype(BF16)
    uk = w_uk[l]
    w_uk_p = _head_tile_cols(jnp.zeros((kv_lora, N_HEADS, QK_ROPE), uk.dtype), uk).astype(BF16)
    w_ukt = jnp.transpose(uk, (2, 1, 0)).reshape(QK_NOPE * N_HEADS, kv_lora).astype(BF16)
    uv = w_uv[l]
    w_uv_p = jnp.concatenate([uv, jnp.zeros((kv_lora, N_HEADS, HEAD_TILE - V_HEAD), uv.dtype)],
                             axis=-1).reshape(kv_lora, N_HEADS * HEAD_TILE).astype(BF16)
    lane = jnp.arange(HEAD_TILE)
    bound = (1.02 * LOG2E * math.sqrt(QK_HEAD)) * jnp.max(jnp.abs(q_norm_g[l])) * jnp.max(jnp.abs(k_norm_g[l]))
    q_one = jnp.where(lane == BIAS_LANE, 1.0, 0.0).astype(F32)[None]
    k_bias = jnp.where(lane == BIAS_LANE, -bound, 0.0).astype(F32)[None]
    v_one = jnp.tile(jnp.where(lane == ONES_LANE, 1.0, 0.0).astype(F32), N_HEADS)[None]
    ukt = jnp.transpose(uk, (1, 2, 0))
    w_abs = jnp.concatenate([jnp.zeros((N_HEADS, QK_ROPE, kv_lora), uk.dtype), ukt,
                             jnp.zeros((N_HEADS, HEAD_TILE - QK_HEAD, kv_lora), uk.dtype)], axis=1).astype(BF16)
    lane_ids = jnp.arange(HEAD_TILE)[None, :, None]
    col_ids = jnp.arange(N_HEADS * QK_ROPE)[None, None, :]
    head_ids = jnp.arange(N_HEADS)[:, None, None]
    w_sel = ((lane_ids < QK_ROPE) & (col_ids == head_ids * QK_ROPE + lane_ids)).astype(BF16)
    eye_h = jnp.eye(N_HEADS, dtype=uv.dtype)
    w_uv_bd = jnp.einsum("chd,hg->hcgd", uv, eye_h).reshape(N_HEADS * kv_lora, N_HEADS * V_HEAD).astype(BF16)
    g_k = _gain_tile(k_norm_g[l])
    return dict(
        g_mix=norm_mix_g[l][None].astype(F32), w_in=w_in_p, g_qa=q_a_norm_g[l][None].astype(F32), w_uq=w_uq_p,
        g_kva=kv_a_norm_g[l][None].astype(F32), g_q=_gain_tile(q_norm_g[l]), g_k=g_k,
        w_uk=w_uk_p, w_uv_p=w_uv_p, w_uv_bd=w_uv_bd, q_one=q_one, k_bias=k_bias, v_one=v_one, score_bound=bound,
        w_ukt=w_ukt, w_abs=w_abs, w_sel=w_sel, w_conv_out=w_conv_out[l].astype(BF16),
        w_o_mla=w_o_mla[l].astype(BF16),
        w_out=w_out[l].astype(BF16), g_ffn=norm_ffn_g[l][None].astype(F32), w_gate=w_gate[l].astype(BF16),
        w_up=w_up[l].astype(BF16), w_down=w_down[l].astype(BF16))


ROW_TILE = 512
SUB_ROWS = 256
PAGED_CHUNK = 4096


def _tile_plan(seq, n_dec_tokens, past):
    row = min(ROW_TILE, seq)
    tk = min(PAGED_CHUNK, past)
    assert seq % row == 0 and row % SUB_ROWS == 0 and n_dec_tokens % 2 == 0
    assert past % (tk * CHUNK_UNROLL) == 0 and tk % PAGE_SIZE == 0
    return dict(row=row, sub=SUB_ROWS, dec=n_dec_tokens, dec_sub=n_dec_tokens // 2, tk=tk)


def kernel(x_prompt, x_sample, cache_kv_latent, cache_k_rope, state_conv, page_table, norm_mix_g, w_in,
           q_a_norm_g, w_uq, kv_a_norm_g, w_uk, w_uv, q_norm_g, k_norm_g, w_o_mla, conv_w, conv_b, conv_ln_g,
           conv_ln_b, w_conv_out, w_out, norm_ffn_g, w_gate, w_up, w_down):
    batch, seq, d_model = x_prompt.shape
    nb, t_new, _ = x_sample.shape
    depth = w_in.shape[0]
    n_pages = page_table.shape[1]
    past = n_pages * PAGE_SIZE
    n_state = state_conv.shape[2]
    d_conv = state_conv.shape[3]
    kv_lora = cache_kv_latent.shape[-1]

    plan = _tile_plan(seq, nb * t_new, past)
    tabs_p = _rope_tables(jnp.arange(seq))
    tabs_s = _rope_tables(jnp.tile(past + jnp.arange(t_new), nb))

    yp = x_prompt.reshape(batch * seq, d_model)
    ys = x_sample.reshape(nb * t_new, d_model)
    outs = [[] for _ in range(6)]
    for l in range(depth):
        wts = _layer_weights(l, norm_mix_g, w_in, q_a_norm_g, w_uq, kv_a_norm_g, w_uk, w_uv, q_norm_g,
                             k_norm_g, w_o_mla, w_conv_out, w_out, norm_ffn_g, w_gate, w_up, w_down)
        cb, lg, lb = (a[l][None].astype(F32) for a in (conv_b, conv_ln_g, conv_ln_b))

        u, lat, kpe, gates, q, k, v = _in_proj(yp, tabs_p, wts, tm=plan["row"], sub_rows=plan["sub"], decode=False)
        attn = lax.cond(wts["score_bound"] <= FIXED_SHIFT_MAX_BOUND,
                        functools.partial(_flash_attention, batch=batch, tq=plan["row"], heads_per_step=N_HEADS,
                                          online=False),
                        functools.partial(_flash_attention, batch=batch, tq=plan["row"], heads_per_step=2,
                                          online=True), q, k, v)
        y_act = _conv_prompt(u, conv_w[l].astype(F32), cb, lg, lb, batch=batch, ts=plan["row"])
        yp = _out_ffn(yp, attn, y_act, gates, wts, tm=plan["row"], latent_attn=False)
        outs[0].append(lat.reshape(batch, seq, kv_lora))
        outs[1].append(jnp.swapaxes(kpe, 1, 2))
        outs[2].append(u.reshape(batch, seq, d_conv)[:, seq - n_state:])

        u, lat, kpe, gates, qa, qpe = _in_proj(ys, tabs_s, wts, tm=plan["dec"], sub_rows=plan["dec_sub"], decode=True)
        u_t = jnp.transpose(u.reshape(nb, t_new, d_conv), (1, 0, 2))
        state_t = jnp.transpose(state_conv[l], (1, 0, 2))
        y_act = _conv_decode(state_t, u_t, conv_w[l].astype(F32), cb, lg, lb)
        y_act = jnp.transpose(y_act, (1, 0, 2)).reshape(nb * t_new, d_conv)
        attn = _paged_attention(page_table, qa, qpe, lat, kpe, cache_kv_latent, jnp.swapaxes(cache_k_rope, 2, 3),
                                wts["w_ukt"], layer=l, tk=plan["tk"])
        ys = _out_ffn(ys, attn, y_act, gates, wts, tm=plan["dec"], latent_attn=True)
        outs[3].append(lat.reshape(nb, t_new, kv_lora))
        outs[4].append(kpe.reshape(nb, t_new, QK_ROPE))
        outs[5].append(jnp.transpose(jnp.concatenate([state_t[t_new:], u_t], axis=0), (1, 0, 2)))

    return (yp.reshape(batch, seq, d_model), ys.reshape(nb, t_new, d_model),
            *(jnp.stack(o) for o in outs))
```

```python
import functools
import math

import jax
import jax.numpy as jnp
from jax import lax
from jax.experimental import pallas as pl
from jax.experimental.pallas import tpu as pltpu

N_HEADS = 8
QK_NOPE = 64
QK_ROPE = 32
ROPE_HALF = QK_ROPE // 2
QK_HEAD = QK_NOPE + QK_ROPE
V_HEAD = 64
ROPE_THETA = 10000.0
EPS = 1e-6
PAGE_SIZE = 128

LANES = 128
HEAD_TILE = LANES
NOPE_OFF = QK_ROPE
VMEM_LIMIT_BYTES = 56 * 1024 * 1024

BIAS_LANE = QK_HEAD
ONES_LANE = V_HEAD
LOG2E = 1.4426950408889634
FIXED_SHIFT_MAX_BOUND = 40.0

F32 = jnp.float32
BF16 = jnp.bfloat16
NEG_BIG = -1e30


def _dot(a, b):
    return jnp.dot(a, b, preferred_element_type=F32)


def _dot_nt(a, b):
    return lax.dot_general(a, b, (((1,), (1,)), ((), ())), preferred_element_type=F32)


def _rms_scale(x):
    return lax.rsqrt(jnp.mean(x * x, axis=-1, keepdims=True) + EPS)


def _compiler_params(n_grid_axes):
    return pltpu.CompilerParams(dimension_semantics=("arbitrary",) * n_grid_axes,
                                vmem_limit_bytes=VMEM_LIMIT_BYTES)


def _const_spec(shape, single_buffer=False):
    nd = len(shape)
    if single_buffer:
        return pl.BlockSpec(shape, lambda *_: (0,) * nd, pipeline_mode=pl.Buffered(1))
    return pl.BlockSpec(shape, lambda *_: (0,) * nd)


def _rope_tile(t, c_tab, sl_tab, sr_tab):
    return t * c_tab + pltpu.roll(t, ROPE_HALF, 1) * sr_tab + pltpu.roll(t, LANES - ROPE_HALF, 1) * sl_tab


N_ROW_INPUTS = 4
N_WEIGHT_INPUTS = 16
KPE_OUT = 2


def _in_proj_kernel(*refs, sub_rows, **kw):
    n_in = N_ROW_INPUTS + N_WEIGHT_INPUTS
    rows, weights, outs = refs[:N_ROW_INPUTS], refs[N_ROW_INPUTS:n_in], list(refs[n_in:])
    for r0 in range(0, rows[0].shape[0], sub_rows):
        view = lambda ref: ref.at[pl.ds(r0, sub_rows)]
        lane_view = lambda ref: ref.at[0, :, pl.ds(r0, sub_rows)]
        sub_outs = [lane_view(o) if (n == KPE_OUT and not kw["decode"]) else view(o) for n, o in enumerate(outs)]
        _in_proj_rows(*map(view, rows), *weights, *sub_outs, **kw)


def _in_proj_rows(x_ref, ctab_ref, sltab_ref, srtab_ref, gmix_ref, w_in_ref, w_kpe_ref, w_gates_ref, gqa_ref, w_uq_ref,
                  gkva_ref, gq_ref, gk_ref, qone_ref, kbias_ref, vone_ref, w_uk_ref, w_uv_ref,
                  wabs_ref, sel_ref, u_ref, lat_ref, kpe_ref, gates_ref, *attn_refs,
                  d_conv, q_lora, kv_lora, d_model, decode):
    x = x_ref[...]
    hb = (x * _rms_scale(x) * gmix_ref[...]).astype(BF16)
    c_tab, sl_tab, sr_tab = ctab_ref[...], sltab_ref[...], srtab_ref[...]

    o_cq = 2 * d_conv
    o_ckv = o_cq + q_lora
    o_kpe = o_ckv + kv_lora

    ag = _dot(hb, w_in_ref[:, 0:o_cq])
    u_ref[...] = ag[:, :d_conv] * jax.nn.sigmoid(ag[:, d_conv:])

    cq = _dot(hb, w_in_ref[:, o_cq:o_ckv])
    cqn = (cq * _rms_scale(cq) * gqa_ref[...]).astype(BF16)
    q = _dot(cqn, w_uq_ref[...])
    q_scale = LOG2E / math.sqrt(QK_HEAD)
    qpe = None
    for h in range(N_HEADS):
        sl = slice(h * HEAD_TILE, (h + 1) * HEAD_TILE)
        qh = _rope_tile(q[:, sl], c_tab, sl_tab, sr_tab)
        ss = jnp.sum(qh * qh, axis=-1, keepdims=True)
        qh = qh * lax.rsqrt(ss * (1.0 / QK_HEAD) + EPS) * (gq_ref[...] * q_scale)
        if decode:
            qa_ref, qpe_ref = attn_refs
            qg = (qh * gk_ref[...]).astype(BF16)
            qa_ref[:, h * kv_lora:(h + 1) * kv_lora] = _dot(qg, wabs_ref[h]).astype(qa_ref.dtype)
            part = _dot(qg, sel_ref[h])
            qpe = part if qpe is None else qpe + part
        else:
            attn_refs[0][:, sl] = (qh + qone_ref[...]).astype(attn_refs[0].dtype)
    if decode:
        qpe_ref[...] = qpe.astype(qpe_ref.dtype)

    ckv = _dot(hb, w_in_ref[:, o_ckv:o_kpe])
    lat = ckv * _rms_scale(ckv) * gkva_ref[...]
    lat_ref[...] = lat
    kpe = _rope_tile(_dot(hb, w_kpe_ref[...]), c_tab, sl_tab, sr_tab)
    if decode:
        kpe_ref[...] = kpe[:, :QK_ROPE]
    else:
        kpe_ref[...] = kpe.T[:QK_ROPE]

    for g in range(2):
        sl = slice(g * d_model, (g + 1) * d_model)
        gate = jax.nn.sigmoid(_dot(hb, w_gates_ref[:, g * d_model:(g + 1) * d_model]))
        gates_ref[:, sl] = gate.astype(gates_ref.dtype)

    if not decode:
        _, k_ref, v_ref = attn_refs
        latb = lat.astype(BF16)
        v_ref[...] = (_dot(latb, w_uv_ref[...]) + vone_ref[...]).astype(v_ref.dtype)
        kn = _dot(latb, w_uk_ref[...])
        for h in range(N_HEADS):
            sl = slice(h * HEAD_TILE, (h + 1) * HEAD_TILE)
            kh = kn[:, sl] + kpe
            ss = jnp.sum(kh * kh, axis=-1, keepdims=True)
            kh = kh * lax.rsqrt(ss * (1.0 / QK_HEAD) + EPS) * gk_ref[...] + kbias_ref[...]
            k_ref[:, sl] = kh.astype(k_ref.dtype)


def _in_proj(x, tabs, wts, *, tm, sub_rows, decode):
    t, d_model = x.shape
    n_tab_tiles = tabs[0].shape[0] // tm
    d_in_p = wts["w_in"].shape[1]
    q_lora = wts["w_uq"].shape[0]
    kv_lora = wts["w_uk"].shape[0]
    d_conv = (d_in_p - q_lora - kv_lora) // 2
    qw = N_HEADS * HEAD_TILE
    vw = N_HEADS * HEAD_TILE

    row = lambda w: pl.BlockSpec((tm, w), lambda i: (i, 0))
    tab = pl.BlockSpec((tm, LANES), lambda i: (i % n_tab_tiles, 0))
    in_specs = [row(d_model), tab, tab, tab,
                _const_spec((1, d_model)), _const_spec(wts["w_in"].shape), _const_spec(wts["w_kpe"].shape),
                _const_spec(wts["w_gates"].shape),
                _const_spec((1, q_lora)), _const_spec(wts["w_uq"].shape),
                _const_spec((1, kv_lora)), _const_spec((1, HEAD_TILE)), _const_spec((1, HEAD_TILE)),
                _const_spec((1, HEAD_TILE)), _const_spec((1, HEAD_TILE)), _const_spec((1, vw)),
                _const_spec(wts["w_uk"].shape), _const_spec(wts["w_uv_p"].shape),
                _const_spec(wts["w_abs"].shape), _const_spec(wts["w_sel"].shape)]
    out_shape = [jax.ShapeDtypeStruct((t, d_conv), F32), jax.ShapeDtypeStruct((t, kv_lora), F32),
                 jax.ShapeDtypeStruct((t, QK_ROPE), F32), jax.ShapeDtypeStruct((t, 2 * d_model), BF16)]
    out_specs = [row(d_conv), row(kv_lora), row(QK_ROPE), row(2 * d_model)]
    if not decode:
        out_shape[KPE_OUT] = jax.ShapeDtypeStruct((t // (n_tab_tiles * tm), QK_ROPE, n_tab_tiles * tm), F32)
        out_specs[KPE_OUT] = pl.BlockSpec((1, QK_ROPE, tm), lambda i: (i // n_tab_tiles, 0, i % n_tab_tiles))
    if decode:
        widths = [N_HEADS * kv_lora, N_HEADS * QK_ROPE]
    else:
        widths = [qw, qw, vw]
    out_shape += [jax.ShapeDtypeStruct((t, w), BF16) for w in widths]
    out_specs += [row(w) for w in widths]
    kern = functools.partial(_in_proj_kernel, d_conv=d_conv, q_lora=q_lora, kv_lora=kv_lora,
                             d_model=d_model, decode=decode, sub_rows=sub_rows)
    return pl.pallas_call(
        kern, grid=(t // tm,), in_specs=in_specs, out_specs=out_specs, out_shape=out_shape,
        compiler_params=_compiler_params(1),
        name="in_proj_decode" if decode else "in_proj_prompt",
    )(x, *tabs, wts["g_mix"], wts["w_in"], wts["w_kpe"], wts["w_gates"], wts["g_qa"], wts["w_uq"], wts["g_kva"], wts["g_q"], wts["g_k"],
      wts["q_one"], wts["k_bias"], wts["v_one"], wts["w_uk"], wts["w_uv_p"], wts["w_abs"], wts["w_sel"])


def _ln_silu(y, g, b):
    mu = jnp.mean(y, axis=-1, keepdims=True)
    yc = y - mu
    yn = yc * lax.rsqrt(jnp.mean(yc * yc, axis=-1, keepdims=True) + EPS) * g + b
    return yn * jax.nn.sigmoid(yn)


HALO = 32


SUBLANES = 8


def _conv_prompt_kernel(cur_ref, prev_ref, w_ref, b_ref, lng_ref, lnb_ref, y_ref, ext_ref, sh_ref, *, rows):
    ts = cur_ref.shape[1]
    width = w_ref.shape[0]
    ext_ref[HALO:, :] = cur_ref[0]
    prev = prev_ref[0]
    ext_ref[0:HALO, :] = jnp.where(pl.program_id(1) > 0, prev, jnp.zeros_like(prev))
    span = sh_ref.shape[1]
    for r in range(1, SUBLANES):
        sh_ref[r - 1] = ext_ref[r:r + span, :]
    base = HALO - (width - 1)
    for r0 in range(0, ts, rows):
        acc = None
        for k in range(width):
            kk = base + k
            start = r0 + (kk // SUBLANES) * SUBLANES
            if kk % SUBLANES == 0:
                window = ext_ref[start:start + rows, :]
            else:
                window = sh_ref[kk % SUBLANES - 1, start:start + rows, :]
            term = window * jnp.tile(w_ref[k], (rows // SUBLANES, 1))
            acc = term if acc is None else acc + term
        y = _ln_silu(acc + b_ref[...], lng_ref[...], lnb_ref[...])
        y_ref[r0:r0 + rows, :] = y.astype(y_ref.dtype)


def _conv_prompt(u, conv_w, conv_b, ln_g, ln_b, *, batch, ts, rows=64):
    t, d_conv = u.shape
    seq = t // batch
    width = conv_w.shape[0]
    assert width - 1 <= HALO and ts % HALO == 0 and seq % ts == 0 and HALO % SUBLANES == 0
    u3 = u.reshape(batch, seq, d_conv)
    nst = seq // ts
    span = ts + HALO - SUBLANES
    conv_w = jnp.broadcast_to(conv_w[:, None, :], (width, SUBLANES, d_conv))
    kern = functools.partial(_conv_prompt_kernel, rows=rows)
    return pl.pallas_call(
        kern, grid=(batch, nst),
        in_specs=[pl.BlockSpec((1, ts, d_conv), lambda b, i: (b, i, 0)),
                  pl.BlockSpec((1, HALO, d_conv), lambda b, i: (b, jnp.maximum(i * (ts // HALO) - 1, 0), 0)),
                  _const_spec(conv_w.shape), _const_spec((1, d_conv)), _const_spec((1, d_conv)),
                  _const_spec((1, d_conv))],
        out_specs=pl.BlockSpec((ts, d_conv), lambda b, i: (b * nst + i, 0)),
        out_shape=jax.ShapeDtypeStruct((t, d_conv), BF16),
        scratch_shapes=[pltpu.VMEM((ts + HALO, d_conv), F32), pltpu.VMEM((SUBLANES - 1, span, d_conv), F32)],
        compiler_params=_compiler_params(2),
        name="conv_prompt",
    )(u3, u3, conv_w, conv_b, ln_g, ln_b)


def _conv_decode_kernel(state_ref, unew_ref, w_ref, b_ref, lng_ref, lnb_ref, y_ref):
    n_state = state_ref.shape[0]
    for t in range(unew_ref.shape[0]):
        acc = None
        for k in range(w_ref.shape[0]):
            j = t + k
            src = state_ref[j] if j < n_state else unew_ref[j - n_state]
            term = src * w_ref[k:k + 1, :]
            acc = term if acc is None else acc + term
        y_ref[t] = _ln_silu(acc + b_ref[...], lng_ref[...], lnb_ref[...]).astype(y_ref.dtype)


def _conv_decode(state_t, u_new_t, conv_w, conv_b, ln_g, ln_b):
    t_new, nb, d_conv = u_new_t.shape
    assert state_t.shape[0] == conv_w.shape[0] - 1
    full = lambda a: _const_spec(a.shape)
    return pl.pallas_call(
        _conv_decode_kernel, grid=(1,),
        in_specs=[full(state_t), full(u_new_t), full(conv_w), _const_spec((1, d_conv)),
                  _const_spec((1, d_conv)), _const_spec((1, d_conv))],
        out_specs=_const_spec((t_new, nb, d_conv)),
        out_shape=jax.ShapeDtypeStruct((t_new, nb, d_conv), BF16),
        compiler_params=_compiler_params(1),
        name="conv_decode",
    )(state_t, u_new_t, conv_w, conv_b, ln_g, ln_b)


def _flash_kernel(qi_tab, ki_tab, q_ref, k_ref, v_ref, o_ref, acc_ref, *m_refs, online):
    step = pl.program_id(2)
    qi = qi_tab[step]
    ki = ki_tab[step]
    tq = q_ref.shape[1]
    tk = k_ref.shape[1]
    n_heads = acc_ref.shape[0]

    @pl.when(ki == 0)
    def _():
        acc_ref[...] = jnp.zeros_like(acc_ref)
        if online:
            m_refs[0][...] = jnp.full_like(m_refs[0], NEG_BIG)

    def accumulate(masked):
        if masked:
            visible = (lax.broadcasted_iota(jnp.int32, (tq, tk), 1)
                       <= lax.broadcasted_iota(jnp.int32, (tq, tk), 0))
        for hh in range(n_heads):
            sl = slice(hh * HEAD_TILE, (hh + 1) * HEAD_TILE)
            s = _dot_nt(q_ref[0, :, sl], k_ref[0, :, sl])
            if masked:
                s = jnp.where(visible, s, NEG_BIG)
            if online:
                m_old = m_refs[0][hh]
                m_new = jnp.maximum(m_old, jnp.max(s, axis=-1, keepdims=True))
                m_refs[0][hh] = m_new
                p = jnp.exp2(s - m_new).astype(BF16)
                acc_ref[hh] = jnp.exp2(m_old - m_new) * acc_ref[hh] + _dot(p, v_ref[0, :, sl])
            else:
                acc_ref[hh] += _dot(jnp.exp2(s).astype(BF16), v_ref[0, :, sl])

    @pl.when(ki < qi)
    def _():
        accumulate(False)

    @pl.when(ki == qi)
    def _():
        accumulate(True)
        lane = lax.broadcasted_iota(jnp.int32, (tq, HEAD_TILE), 1)
        for pair in range(n_heads // 2):
            a0, a1 = acc_ref[2 * pair], acc_ref[2 * pair + 1]
            h0 = a0 / a0[:, ONES_LANE:ONES_LANE + 1]
            h1 = a1 / a1[:, ONES_LANE:ONES_LANE + 1]
            o_ref[0, :, pair * HEAD_TILE:(pair + 1) * HEAD_TILE] = jnp.where(
                lane < V_HEAD, h0, pltpu.roll(h1, V_HEAD, 1)).astype(o_ref.dtype)


def _flash_attention(q, k, v, *, batch, tq, heads_per_step, online):
    t = q.shape[0]
    seq = t // batch
    nq = seq // tq
    assert 2 * V_HEAD == HEAD_TILE and heads_per_step % 2 == 0 and N_HEADS % heads_per_step == 0
    q3, k3, v3 = (a.reshape(batch, seq, a.shape[1]) for a in (q, k, v))
    qi_list = [qi for qi in range(nq) for _ in range(qi + 1)]
    ki_list = [ki for qi in range(nq) for ki in range(qi + 1)]
    qi_tab = jnp.asarray(qi_list, jnp.int32)
    ki_tab = jnp.asarray(ki_list, jnp.int32)
    pair_tile = lambda tab: pl.BlockSpec((1, tq, heads_per_step * HEAD_TILE),
                                         lambda b, hp, s, qt, kt: (b, (qt if tab == "q" else kt)[s], hp))
    scratch = [pltpu.VMEM((heads_per_step, tq, HEAD_TILE), F32)]
    if online:
        scratch.append(pltpu.VMEM((heads_per_step, tq, 1), F32))
    grid_spec = pltpu.PrefetchScalarGridSpec(
        num_scalar_prefetch=2, grid=(batch, N_HEADS // heads_per_step, len(qi_list)),
        in_specs=[pair_tile("q"), pair_tile("k"), pair_tile("k")],
        out_specs=pl.BlockSpec((1, tq, heads_per_step * V_HEAD), lambda b, hp, s, qt, kt: (b, qt[s], hp)),
        scratch_shapes=scratch)
    out = pl.pallas_call(
        functools.partial(_flash_kernel, online=online), grid_spec=grid_spec,
        out_shape=jax.ShapeDtypeStruct((batch, seq, N_HEADS * V_HEAD), BF16),
        compiler_params=_compiler_params(3),
        name="flash_prompt_online" if online else "flash_prompt",
    )(qi_tab, ki_tab, q3, k3, v3)
    return out.reshape(t, N_HEADS * V_HEAD)


FETCH_SLOTS = 3
CHUNK_UNROLL = 1


def _paged_kernel(pt_ref, qa_ref, qpe_ref, latn_ref, kpen_ref, wukt_ref, lat_hbm, kpet_hbm,
                  o_ref, lat_buf, kpet_buf, sems, cb_ref, cbt_ref, cbn_ref, kpn_ref, s_ref, lhs_ref, *,
                  layer, n_pages, tk):
    b = pl.program_id(0)
    nb = pl.num_programs(0)
    past = n_pages * PAGE_SIZE
    n_chunks = past // tk
    t_new = latn_ref.shape[1]
    rows = qa_ref.shape[1]

    def page_copies(seq, slot, j):
        page = pt_ref[seq, j]
        dst = pl.ds(j * PAGE_SIZE, PAGE_SIZE)
        return (pltpu.make_async_copy(lat_hbm.at[layer, page], lat_buf.at[slot, dst], sems.at[slot, 0]),
                pltpu.make_async_copy(kpet_hbm.at[layer, page], kpet_buf.at[slot, j], sems.at[slot, 1]))

    def start_fetch(seq, slot, pages=None):
        for j in (range(n_pages) if pages is None else pages):
            lat_cp, kpe_cp = page_copies(seq, slot, j)
            lat_cp.start(priority=j % 2)
            kpe_cp.start(priority=(j + 1) % 2)

    def wait_fetch(slot):
        pltpu.make_async_copy(lat_buf.at[slot], lat_buf.at[slot], sems.at[slot, 0]).wait()
        pltpu.make_async_copy(kpet_buf.at[slot], kpet_buf.at[slot], sems.at[slot, 1]).wait()

    slot = b % FETCH_SLOTS

    @pl.when(b == 0)
    def _():
        for s in range(FETCH_SLOTS - 1):
            start_fetch(jnp.minimum(s, nb - 1), s)
        lhs_ref[0:N_HEADS * QK_NOPE, :] = wukt_ref[...]
        cbn_ref[...] = jnp.zeros_like(cbn_ref)
        kpn_ref[...] = jnp.zeros_like(kpn_ref)

    lhs_ref[N_HEADS * QK_NOPE:, :] = qa_ref[0]
    qpe = qpe_ref[0]
    pages_per_chunk = tk // PAGE_SIZE

    def scores(cbt, ss_pe, s_pe):
        r = _dot(lhs_ref[...], cbt)
        n = cbt.shape[1]
        kn = r[:N_HEADS * QK_NOPE]
        ss = jnp.sum((kn * kn).reshape(QK_NOPE, N_HEADS, n), axis=0) + ss_pe
        inv = lax.rsqrt(ss * (1.0 / QK_HEAD) + EPS)
        return (r[N_HEADS * QK_NOPE:] + s_pe) * jnp.concatenate([inv] * t_new, axis=0)

    def lane_tiles(a):
        return [a[:, i:i + LANES] for i in range(0, a.shape[1], LANES)]

    cbn_ref[0:t_new, :] = latn_ref[0].astype(BF16)
    kpn_ref[0:t_new, :] = kpen_ref[0]
    cbn = cbn_ref[...]
    cbn_t = cbn.astype(F32).T.astype(BF16)
    kpn_t = kpn_ref[...].T
    col = lax.broadcasted_iota(jnp.int32, (rows, LANES), 1)
    qrow = lax.broadcasted_iota(jnp.int32, (rows, LANES), 0) // N_HEADS
    causal_new = col <= qrow

    wait_fetch(slot)

    def chunk_group(g, m):
        for j in range(CHUNK_UNROLL):
            c = g * CHUNK_UNROLL + j
            st = pl.multiple_of(c * tk, tk)
            cb = lat_buf[slot, pl.ds(st, tk), :].astype(BF16)
            cb_ref[c, 0:tk, :] = cb
            cb_ref[c, tk:, :] = cbn
            kpt = jnp.concatenate([kpet_buf[slot, c * pages_per_chunk + i] for i in range(pages_per_chunk)]
                                  + [kpn_t], axis=1)
            cbt_ref[j, :, 0:tk] = cb.T
            cbt_ref[j, :, tk:] = cbn_t
            sc = scores(cbt_ref[j], jnp.sum(kpt * kpt, axis=0, keepdims=True), _dot(qpe, kpt.astype(BF16)))
            new_tile = jnp.where(jnp.logical_and(causal_new, c == n_chunks - 1), sc[:, tk:], NEG_BIG)
            sc = jnp.concatenate([sc[:, :tk], new_tile], axis=1)
            s_ref[c] = sc
            m = functools.reduce(jnp.maximum, [m] + lane_tiles(sc))
        return m

    m = lax.fori_loop(0, n_chunks // CHUNK_UNROLL, chunk_group, jnp.full((rows, LANES), NEG_BIG, F32))
    m = jnp.max(m, axis=-1, keepdims=True)

    ahead = FETCH_SLOTS - 1
    nxt = jnp.minimum(b + ahead, nb - 1)
    nxt_slot = (b + ahead) % FETCH_SLOTS
    pages_per_pv = n_pages // n_chunks
    l = jnp.zeros((rows, LANES), F32)
    o = jnp.zeros((rows, cb_ref.shape[2]), F32)
    for c in range(n_chunks):
        start_fetch(nxt, nxt_slot, range(c * pages_per_pv, (c + 1) * pages_per_pv))
        p = jnp.exp2(s_ref[c] - m)
        l = functools.reduce(jnp.add, [l] + lane_tiles(p))
        o = o + _dot(p.astype(BF16), cb_ref[c])
    l = jnp.sum(l, axis=-1, keepdims=True)
    o_ref[0] = (o / l).astype(o_ref.dtype)

    @pl.when(b == nb - 1)
    def _():
        for d in range(1, FETCH_SLOTS):
            wait_fetch((b + d) % FETCH_SLOTS)


def _paged_attention(page_table, qa, qpe, lat_new, kpe_new, cache_lat, cache_kpe_t, wukt, *, layer, tk):
    nb, n_pages = page_table.shape
    assert tk % PAGE_SIZE == 0 and cache_kpe_t.shape[2:] == (QK_ROPE, PAGE_SIZE)
    assert (n_pages * PAGE_SIZE) % (tk * CHUNK_UNROLL) == 0
    t_new = qa.shape[0] // nb
    kv_lora = cache_lat.shape[-1]
    past = n_pages * PAGE_SIZE
    rows = t_new * N_HEADS
    qa3 = qa.reshape(nb, rows, kv_lora)
    qpe3 = qpe.reshape(nb, rows, QK_ROPE)
    ln3 = lat_new.reshape(nb, t_new, kv_lora)
    kn3 = kpe_new.reshape(nb, t_new, QK_ROPE)
    per_seq = lambda r, w: pl.BlockSpec((1, r, w), lambda b, pt: (b, 0, 0))
    const = lambda a: pl.BlockSpec(a.shape, lambda b, pt: (0,) * a.ndim)
    grid_spec = pltpu.PrefetchScalarGridSpec(
        num_scalar_prefetch=1, grid=(nb,),
        in_specs=[per_seq(rows, kv_lora), per_seq(rows, QK_ROPE), per_seq(t_new, kv_lora),
                  per_seq(t_new, QK_ROPE), const(wukt), pl.BlockSpec(memory_space=pl.ANY),
                  pl.BlockSpec(memory_space=pl.ANY)],
        out_specs=per_seq(rows, kv_lora),
        scratch_shapes=[pltpu.VMEM((FETCH_SLOTS, past, kv_lora), F32),
                        pltpu.VMEM((FETCH_SLOTS, n_pages, QK_ROPE, PAGE_SIZE), F32),
                        pltpu.SemaphoreType.DMA((FETCH_SLOTS, 2)),
                        pltpu.VMEM((past // tk, tk + LANES, kv_lora), BF16),
                        pltpu.VMEM((CHUNK_UNROLL, kv_lora, tk + LANES), BF16),
                        pltpu.VMEM((LANES, kv_lora), BF16),
                        pltpu.VMEM((LANES, QK_ROPE), F32), pltpu.VMEM((past // tk, rows, tk + LANES), F32),
                        pltpu.VMEM((N_HEADS * QK_NOPE + rows, kv_lora), BF16)])
    kern = functools.partial(_paged_kernel, layer=layer, n_pages=n_pages, tk=tk)
    out = pl.pallas_call(
        kern, grid_spec=grid_spec,
        out_shape=jax.ShapeDtypeStruct((nb, rows, kv_lora), BF16),
        compiler_params=_compiler_params(1),
        name="paged_decode",
    )(page_table, qa3, qpe3, ln3, kn3, wukt, cache_lat, cache_kpe_t)
    return out.reshape(nb * t_new, N_HEADS * kv_lora)


FF_CHUNK = 256


def _out_ffn_kernel(x_ref, attn_ref, y_ref, gates_ref, wco_ref, womla_ref, wout_ref, gffn_ref,
                    wg_ref, wu_ref, wd_ref, *rest, latent_attn):
    if latent_attn:
        wuv_ref, o_ref, act_ref = rest
        attn = _dot(attn_ref[...], wuv_ref[...]).astype(BF16)
    else:
        o_ref, act_ref = rest
        attn = attn_ref[...]
    y_cur = y_ref[...]
    d_model = x_ref.shape[1]
    d_ff = wg_ref.shape[1]
    half = d_model // 2
    for c0 in (0, half):
        sl = slice(c0, c0 + half)
        merged = (gates_ref[:, sl] * _dot(y_cur, wco_ref[:, sl])
                  + gates_ref[:, d_model + c0:d_model + c0 + half] * _dot(attn, womla_ref[:, sl]))
        act_ref[:, sl] = merged.astype(BF16)
    o_ref[...] = x_ref[...] + _dot(act_ref[:, :d_model], wout_ref[...])
    x1 = o_ref[...]
    h2 = (x1 * _rms_scale(x1) * gffn_ref[...]).astype(BF16)
    for c0 in range(0, d_ff, FF_CHUNK):
        sl = slice(c0, c0 + FF_CHUNK)
        g = _dot(h2, wg_ref[:, sl])
        act_ref[:, sl] = (g * jax.nn.sigmoid(g) * _dot(h2, wu_ref[:, sl])).astype(BF16)
    o_ref[...] += _dot(act_ref[...], wd_ref[...])


def _out_ffn(x, attn, y_act, gates, wts, *, tm, latent_attn):
    t, d_model = x.shape
    d_ff = wts["w_gate"].shape[1]
    assert d_ff % FF_CHUNK == 0 and d_ff >= d_model
    row = lambda w: pl.BlockSpec((tm, w), lambda i: (i, 0))
    wspec = lambda a: _const_spec(a.shape, single_buffer=True)
    weights = [wts["w_conv_out"], wts["w_o_mla"], wts["w_out"], wts["g_ffn"], wts["w_gate"], wts["w_up"],
               wts["w_down"]] + ([wts["w_uv_bd"]] if latent_attn else [])
    return pl.pallas_call(
        functools.partial(_out_ffn_kernel, latent_attn=latent_attn), grid=(t // tm,),
        in_specs=[row(d_model), row(attn.shape[1]), row(y_act.shape[1]), row(2 * d_model)]
                 + [wspec(w) for w in weights],
        out_specs=row(d_model),
        out_shape=jax.ShapeDtypeStruct((t, d_model), F32),
        scratch_shapes=[pltpu.VMEM((tm, d_ff), BF16)],
        compiler_params=_compiler_params(1),
        name="out_ffn_decode" if latent_attn else "out_ffn_prompt",
    )(x, attn, y_act, gates, *weights)


def _head_tile_cols(w_rope, w_nope):
    lead = w_nope.shape[:-2]
    pad = jnp.zeros(lead + (N_HEADS, HEAD_TILE - QK_HEAD), w_nope.dtype)
    tile = jnp.concatenate([w_rope, w_nope, pad], axis=-1)
    return tile.reshape(lead + (N_HEADS * HEAD_TILE,))


def _gain_tile(g_compact):
    g = g_compact.astype(F32)
    return jnp.concatenate([g[QK_NOPE:], g[QK_NOPE:], g[:QK_NOPE], jnp.zeros((HEAD_TILE - QK_HEAD,), F32)])[None]


def _rope_tables(pos):
    inv_freq = 1.0 / (ROPE_THETA ** (jnp.arange(0, QK_ROPE, 2, dtype=F32) / QK_ROPE))
    freq_tile = jnp.concatenate([inv_freq, inv_freq, jnp.zeros((LANES - QK_ROPE,), F32)])
    ang = pos.astype(F32)[:, None] * freq_tile[None, :]
    c, s = jnp.cos(ang), jnp.sin(ang)
    lane = jnp.arange(LANES)[None, :]
    c_tab = jnp.where(lane < QK_HEAD, c, 0.0)
    sl_tab = jnp.where(lane < ROPE_HALF, -s, 0.0)
    sr_tab = jnp.where((lane >= ROPE_HALF) & (lane < QK_ROPE), s, 0.0)
    return c_tab, sl_tab, sr_tab


def _layer_weights(l, norm_mix_g, w_in, q_a_norm_g, w_uq, kv_a_norm_g, w_uk, w_uv, q_norm_g, k_norm_g,
                   w_o_mla, w_conv_out, w_out, norm_ffn_g, w_gate, w_up, w_down):
    d_model = w_in.shape[1]
    kv_lora = w_uk.shape[1]
    q_lora = w_uq.shape[1]
    d_conv = w_conv_out.shape[1]
    o1 = 2 * d_conv
    o2 = o1 + q_lora
    o3 = o2 + kv_lora
    o4 = o3 + QK_ROPE
    wi = w_in[l]
    w_in_p = wi[:, :o3].astype(BF16)
    w_kpe = jnp.pad(wi[:, o3:o4], ((0, 0), (0, HEAD_TILE - QK_ROPE))).astype(BF16)
    w_gates = wi[:, o4:].astype(BF16)
    uq = w_uq[l].reshape(q_lora, N_HEADS, QK_HEAD)
    w_uq_p = _head_tile_cols(uq[..., QK_NOPE:], uq[..., :QK_NOPE]).astype(BF16)
    uk = w_uk[l]
    w_uk_p = _head_tile_cols(jnp.zeros((kv_lora, N_HEADS, QK_ROPE), uk.dtype), uk).astype(BF16)
    w_ukt = jnp.transpose(uk, (2, 1, 0)).reshape(QK_NOPE * N_HEADS, kv_lora).astype(BF16)
    uv = w_uv[l]
    w_uv_p = jnp.concatenate([uv, jnp.zeros((kv_lora, N_HEADS, HEAD_TILE - V_HEAD), uv.dtype)],
                             axis=-1).reshape(kv_lora, N_HEADS * HEAD_TILE).astype(BF16)
    lane = jnp.arange(HEAD_TILE)
    bound = (1.02 * LOG2E * math.sqrt(QK_HEAD)) * jnp.max(jnp.abs(q_norm_g[l])) * jnp.max(jnp.abs(k_norm_g[l]))
    q_one = jnp.where(lane == BIAS_LANE, 1.0, 0.0).astype(F32)[None]
    k_bias = jnp.where(lane == BIAS_LANE, -bound, 0.0).astype(F32)[None]
    v_one = jnp.tile(jnp.where(lane == ONES_LANE, 1.0, 0.0).astype(F32), N_HEADS)[None]
    ukt = jnp.transpose(uk, (1, 2, 0))
    w_abs = jnp.concatenate([jnp.zeros((N_HEADS, QK_ROPE, kv_lora), uk.dtype), ukt,
                             jnp.zeros((N_HEADS, HEAD_TILE - QK_HEAD, kv_lora), uk.dtype)], axis=1).astype(BF16)
    lane_ids = jnp.arange(HEAD_TILE)[None, :, None]
    col_ids = jnp.arange(N_HEADS * QK_ROPE)[None, None, :]
    head_ids = jnp.arange(N_HEADS)[:, None, None]
    w_sel = ((lane_ids < QK_ROPE) & (col_ids == head_ids * QK_ROPE + lane_ids)).astype(BF16)
    eye_h = jnp.eye(N_HEADS, dtype=uv.dtype)
    w_uv_bd = jnp.einsum("chd,hg->hcgd", uv, eye_h).reshape(N_HEADS * kv_lora, N_HEADS * V_HEAD).astype(BF16)
    g_k = _gain_tile(k_norm_g[l])
    return dict(
        g_mix=norm_mix_g[l][None].astype(F32), w_in=w_in_p, w_kpe=w_kpe, w_gates=w_gates, g_qa=q_a_norm_g[l][None].astype(F32), w_uq=w_uq_p,
        g_kva=kv_a_norm_g[l][None].astype(F32), g_q=_gain_tile(q_norm_g[l]), g_k=g_k,
        w_uk=w_uk_p, w_uv_p=w_uv_p, w_uv_bd=w_uv_bd, q_one=q_one, k_bias=k_bias, v_one=v_one, score_bound=bound,
        w_ukt=w_ukt, w_abs=w_abs, w_sel=w_sel, w_conv_out=w_conv_out[l].astype(BF16),
        w_o_mla=w_o_mla[l].astype(BF16),
        w_out=w_out[l].astype(BF16), g_ffn=norm_ffn_g[l][None].astype(F32), w_gate=w_gate[l].astype(BF16),
        w_up=w_up[l].astype(BF16), w_down=w_down[l].astype(BF16))


ROW_TILE = 512
SUB_ROWS = 256
PAGED_CHUNK = 4096


def _tile_plan(seq, n_dec_tokens, past):
    row = min(ROW_TILE, seq)
    tk = min(PAGED_CHUNK, past)
    assert seq % row == 0 and row % SUB_ROWS == 0 and n_dec_tokens % 2 == 0
    assert past % (tk * CHUNK_UNROLL) == 0 and tk % PAGE_SIZE == 0
    return dict(row=row, sub=SUB_ROWS, dec=n_dec_tokens, dec_sub=n_dec_tokens // 2, tk=tk)


def kernel(x_prompt, x_sample, cache_kv_latent, cache_k_rope, state_conv, page_table, norm_mix_g, w_in,
           q_a_norm_g, w_uq, kv_a_norm_g, w_uk, w_uv, q_norm_g, k_norm_g, w_o_mla, conv_w, conv_b, conv_ln_g,
           conv_ln_b, w_conv_out, w_out, norm_ffn_g, w_gate, w_up, w_down):
    batch, seq, d_model = x_prompt.shape
    nb, t_new, _ = x_sample.shape
    depth = w_in.shape[0]
    n_pages = page_table.shape[1]
    past = n_pages * PAGE_SIZE
    n_state = state_conv.shape[2]
    d_conv = state_conv.shape[3]
    kv_lora = cache_kv_latent.shape[-1]

    plan = _tile_plan(seq, nb * t_new, past)
    tabs_p = _rope_tables(jnp.arange(seq))
    tabs_s = _rope_tables(jnp.tile(past + jnp.arange(t_new), nb))

    yp = x_prompt.reshape(batch * seq, d_model)
    ys = x_sample.reshape(nb * t_new, d_model)
    outs = [[] for _ in range(6)]
    for l in range(depth):
        wts = _layer_weights(l, norm_mix_g, w_in, q_a_norm_g, w_uq, kv_a_norm_g, w_uk, w_uv, q_norm_g,
                             k_norm_g, w_o_mla, w_conv_out, w_out, norm_ffn_g, w_gate, w_up, w_down)
        cb, lg, lb = (a[l][None].astype(F32) for a in (conv_b, conv_ln_g, conv_ln_b))

        u, lat, kpe, gates, q, k, v = _in_proj(yp, tabs_p, wts, tm=plan["row"], sub_rows=plan["sub"], decode=False)
        attn = lax.cond(wts["score_bound"] <= FIXED_SHIFT_MAX_BOUND,
                        functools.partial(_flash_attention, batch=batch, tq=plan["row"], heads_per_step=N_HEADS,
                                          online=False),
                        functools.partial(_flash_attention, batch=batch, tq=plan["row"], heads_per_step=2,
                                          online=True), q, k, v)
        y_act = _conv_prompt(u, conv_w[l].astype(F32), cb, lg, lb, batch=batch, ts=plan["row"])
        yp = _out_ffn(yp, attn, y_act, gates, wts, tm=plan["row"], latent_attn=False)
        outs[0].append(lat.reshape(batch, seq, kv_lora))
        outs[1].append(jnp.swapaxes(kpe, 1, 2))
        outs[2].append(u.reshape(batch, seq, d_conv)[:, seq - n_state:])

        u, lat, kpe, gates, qa, qpe = _in_proj(ys, tabs_s, wts, tm=plan["dec"], sub_rows=plan["dec_sub"], decode=True)
        u_t = jnp.transpose(u.reshape(nb, t_new, d_conv), (1, 0, 2))
        state_t = jnp.transpose(state_conv[l], (1, 0, 2))
        y_act = _conv_decode(state_t, u_t, conv_w[l].astype(F32), cb, lg, lb)
        y_act = jnp.transpose(y_act, (1, 0, 2)).reshape(nb * t_new, d_conv)
        attn = _paged_attention(page_table, qa, qpe, lat, kpe, cache_kv_latent, jnp.swapaxes(cache_k_rope, 2, 3),
                                wts["w_ukt"], layer=l, tk=plan["tk"])
        ys = _out_ffn(ys, attn, y_act, gates, wts, tm=plan["dec"], latent_attn=True)
        outs[3].append(lat.reshape(nb, t_new, kv_lora))
        outs[4].append(kpe.reshape(nb, t_new, QK_ROPE))
        outs[5].append(jnp.transpose(jnp.concatenate([state_t[t_new:], u_t], axis=0), (1, 0, 2)))

    return (yp.reshape(batch, seq, d_model), ys.reshape(nb, t_new, d_model),
            *(jnp.stack(o) for o in outs))
```
